```python
import jax, jax.numpy as jnp
from jax import lax
import numpy as np

D_MODEL = 1024
BATCH = 4
SEQ = 8192
DEPTH = 2

N_META = 16
POOL_WINDOWS = (2, 4, 8, 16)
N_POOL_GROUPS = len(POOL_WINDOWS)
POOL_GROUP_DIM = D_MODEL // N_POOL_GROUPS
HEAD_DIM = 64
N_HEADS = D_MODEL // HEAD_DIM
D_FF = ((8 * D_MODEL // 3 + 127) // 128) * 128
CONV_WIDTH = 3
Q_BLOCK = 128
N_A_LAYERS = DEPTH // 2
N_B_LAYERS = DEPTH - N_A_LAYERS
ALPHA = (2.0 * DEPTH) ** 0.25
BETA = (8.0 * DEPTH) ** -0.25
LN_EPS = 1e-5
NEG_INF = -1e30

kernel_name = "yoco_pool_fox_convffn_deepnorm_meta"


def layer_norm(x, g, b):
    xf = x.astype(jnp.float32)
    mu = jnp.mean(xf, axis=-1, keepdims=True)
    xc = xf - mu
    var = jnp.mean(xc * xc, axis=-1, keepdims=True)
    y = xc * lax.rsqrt(var + LN_EPS) * g.astype(jnp.float32) + b.astype(jnp.float32)
    return y.astype(x.dtype)


def multiscale_pool_mixer(h, w_group, scale):
    b_, L, D = h.shape
    G = POOL_GROUP_DIM
    hf = h.astype(jnp.float32)
    cs0 = jnp.pad(jnp.cumsum(hf, axis=1), ((0, 0), (1, 0), (0, 0)))
    t = jnp.arange(1, L + 1, dtype=jnp.float32)
    outs = []
    for g, w in enumerate(POOL_WINDOWS):
        sl = slice(g * G, (g + 1) * G)
        upper = cs0[:, 1:, sl]
        lower = jnp.pad(cs0[:, :L + 1 - w, sl], ((0, 0), (w - 1, 0), (0, 0)))
        count = jnp.minimum(t, float(w))[None, :, None]
        outs.append((upper - lower) / count)
    pooled = jnp.concatenate(outs, axis=-1)
    diff = (pooled - hf).astype(h.dtype).reshape(b_, L, N_POOL_GROUPS, G)
    mixed = jnp.einsum('blgc,gcd->blgd', diff, w_group).reshape(b_, L, D)
    return mixed * scale


def conv_glu_ffn(h, w_in, conv_w, conv_b, w_out):
    L = h.shape[1]
    u = h @ w_in
    up = jnp.pad(u, ((0, 0), (CONV_WIDTH - 1, 0), (0, 0)))
    c = conv_b + sum(conv_w[k] * up[:, k:k + L] for k in range(CONV_WIDTH))
    a, g = jnp.split(c, 2, axis=-1)
    return (jax.nn.silu(a) * g) @ w_out


def padded_layout(L):
    front = (-N_META) % Q_BLOCK
    total = ((front + L + Q_BLOCK - 1) // Q_BLOCK) * Q_BLOCK
    return front, total


def shared_kv(h, w_kv, w_f, b_f):
    b_, L, D = h.shape
    front, Lp = padded_layout(L)
    pad = ((0, 0), (front, Lp - front - L), (0, 0), (0, 0))
    kv = h @ w_kv
    k = jnp.pad(kv[..., :D].reshape(b_, L, N_HEADS, HEAD_DIM), pad).transpose(0, 2, 1, 3)
    v = jnp.pad(kv[..., D:].reshape(b_, L, N_HEADS, HEAD_DIM), pad).transpose(0, 2, 1, 3)
    logf = jax.nn.log_sigmoid((h @ w_f).astype(jnp.float32) + b_f.astype(jnp.float32))
    logf = jnp.pad(logf, ((0, 0), (front, Lp - front - L), (0, 0)))
    c = jnp.cumsum(logf, axis=1).transpose(0, 2, 1)
    return k, v, c


def forgetting_attention(h, w_q, w_o, k, v, c):
    b_, L, D = h.shape
    front, Lp = padded_layout(L)
    nb = Lp // Q_BLOCK
    q = (h @ w_q).reshape(b_, L, N_HEADS, HEAD_DIM)
    q = jnp.pad(q, ((0, 0), (front, Lp - front - L), (0, 0), (0, 0)))
    qb = q.reshape(b_, nb, Q_BLOCK, N_HEADS, HEAD_DIM).transpose(1, 0, 3, 2, 4)
    cq = c.reshape(b_, N_HEADS, nb, Q_BLOCK).transpose(2, 0, 1, 3)
    kpos = jnp.arange(Lp)
    scale = HEAD_DIM ** -0.5

    def block(args):
        i, q_i, cq_i = args
        qpos = i * Q_BLOCK + jnp.arange(Q_BLOCK)
        s = jnp.einsum('bhqd,bhkd->bhqk', q_i, k, preferred_element_type=jnp.float32) * scale
        s = s + cq_i[..., None] - c[:, :, None, :]
        mask = (kpos[None, :] <= qpos[:, None]) & (kpos[None, :] >= front)
        p = jax.nn.softmax(jnp.where(mask, s, NEG_INF), axis=-1)
        return jnp.einsum('bhqk,bhkd->bhqd', p.astype(v.dtype), v)

    o = lax.map(block, (jnp.arange(nb), qb, cq))
    o = o.transpose(1, 0, 3, 2, 4).reshape(b_, Lp, D)[:, front:front + L]
    return o @ w_o


def setup_inputs(seed: int = 0) -> dict:
    key = jax.random.key(seed)
    ks = jax.random.split(key, 16)
    D, F, G, H = D_MODEL, D_FF, POOL_GROUP_DIM, N_HEADS
    nrm = jax.random.normal
    return {
        "x": nrm(ks[0], (BATCH, SEQ, D), jnp.float32),
        "meta": nrm(ks[1], (N_META, D), jnp.float32),
        "pool_w": nrm(ks[2], (N_A_LAYERS, N_POOL_GROUPS, G, G), jnp.float32) * (G ** -0.5) * BETA,
        "pool_scale": 1.0 + 0.02 * nrm(ks[3], (N_A_LAYERS, D), jnp.float32),
        "w_kv": nrm(ks[4], (D, 2 * D), jnp.float32) * (D ** -0.5),
        "w_f": nrm(ks[5], (D, H), jnp.float32) * (D ** -0.5),
        "b_f": jax.random.uniform(ks[6], (H,), jnp.float32, 1.0, 6.0),
        "w_q": nrm(ks[7], (N_B_LAYERS, D, D), jnp.float32) * (D ** -0.5),
        "w_o": nrm(ks[8], (N_B_LAYERS, D, D), jnp.float32) * (D ** -0.5) * BETA,
        "ffn_w_in": nrm(ks[9], (DEPTH, D, 2 * F), jnp.float32) * (D ** -0.5),
        "ffn_conv_w": nrm(ks[10], (DEPTH, CONV_WIDTH, 2 * F), jnp.float32) * (CONV_WIDTH ** -0.5),
        "ffn_conv_b": 0.02 * nrm(ks[11], (DEPTH, 2 * F), jnp.float32),
        "ffn_w_out": nrm(ks[12], (DEPTH, F, D), jnp.float32) * (F ** -0.5) * BETA,
        "ln_g": 1.0 + 0.02 * nrm(ks[13], (DEPTH, 2, D), jnp.float32),
        "ln_b": 0.02 * nrm(ks[14], (DEPTH, 2, D), jnp.float32),
    }


def reference(x, meta, pool_w, pool_scale, w_kv, w_f, b_f, w_q, w_o, ffn_w_in, ffn_conv_w,
              ffn_conv_b, ffn_w_out, ln_g, ln_b):
    b_ = x.shape[0]
    h = jnp.concatenate(
        [jnp.broadcast_to(meta[None].astype(x.dtype), (b_, N_META, D_MODEL)), x], axis=1)
    shared = None
    for i in range(DEPTH):
        if i < N_A_LAYERS:
            mix = multiscale_pool_mixer(h, pool_w[i], pool_scale[i])
        else:
            if i == N_A_LAYERS:
                shared = shared_kv(h, w_kv, w_f, b_f)
            j = i - N_A_LAYERS
            mix = forgetting_attention(h, w_q[j], w_o[j], shared[0], shared[1], shared[2])
        h = layer_norm(ALPHA * h + mix, ln_g[i, 0], ln_b[i, 0])
        ffn = conv_glu_ffn(h, ffn_w_in[i], ffn_conv_w[i], ffn_conv_b[i], ffn_w_out[i])
        h = layer_norm(ALPHA * h + ffn, ln_g[i, 1], ln_b[i, 1])
    return h[:, N_META:]
```

```python
import functools

import jax
import jax.numpy as jnp
from jax import lax
from jax.experimental import pallas as pl
from jax.experimental.pallas import tpu as pltpu

D_MODEL = 1024
DEPTH = 2
N_META = 16
POOL_WINDOWS = (2, 4, 8, 16)
N_POOL_GROUPS = len(POOL_WINDOWS)
POOL_GROUP_DIM = D_MODEL // N_POOL_GROUPS
HEAD_DIM = 64
N_HEADS = D_MODEL // HEAD_DIM
D_FF = ((8 * D_MODEL // 3 + 127) // 128) * 128
CONV_WIDTH = 3
N_A_LAYERS = DEPTH // 2
ALPHA = (2.0 * DEPTH) ** 0.25
LN_EPS = 1e-5
NEG_INF = -1e30

TM = 512
META_ROW0 = TM - N_META
HALO = 16
FFN_CHUNK = 256
N_FFN_CHUNKS = D_FF // FFN_CHUNK
LANES = 128
SUBLANES = 8
VMEM_LIMIT_BYTES = 56 * 1024 * 1024

assert D_FF % FFN_CHUNK == 0


def _layer_norm(y, g, b):
    mu = jnp.mean(y, axis=-1, keepdims=True)
    xc = y - mu
    var = jnp.mean(xc * xc, axis=-1, keepdims=True)
    return xc * lax.rsqrt(var + LN_EPS) * g + b


def _const_spec(shape):
    nd = len(shape)
    return pl.BlockSpec(shape, lambda *_: (0,) * nd, pipeline_mode=pl.Buffered(1))


def _params(n_axes):
    return pltpu.CompilerParams(dimension_semantics=("arbitrary",) * n_axes,
                                vmem_limit_bytes=VMEM_LIMIT_BYTES)


def _pool_ln_kernel(tiles_per_batch, h_ref, halo_ref, w_ref, scale_ref, g_ref, b_ref, o_ref):
    del tiles_per_batch
    i = pl.program_id(0)
    h = h_ref[...]
    halo = halo_ref[...] * (i > 0).astype(jnp.float32)
    row = lax.broadcasted_iota(jnp.int32, (TM, 1), 0)
    t = jnp.where(i == 0, row - (META_ROW0 - 1), HALO)
    outs = []
    G = POOL_GROUP_DIM
    for gi, w in enumerate(POOL_WINDOWS):
        sl = slice(gi * G, (gi + 1) * G)
        hg = h[:, sl]
        s = jnp.concatenate([halo[:, sl], hg], axis=0)
        k = 1
        while k < w:
            s = s + pltpu.roll(s, k, axis=0)
            k *= 2
        count = jnp.clip(t, 1, w).astype(jnp.float32)
        pooled = s[HALO:, :] / count
        diff = (pooled - hg).astype(jnp.bfloat16)
        mixed = jnp.dot(diff, w_ref[gi], preferred_element_type=jnp.float32)
        outs.append(ALPHA * hg + mixed * scale_ref[:, sl])
    y = jnp.concatenate(outs, axis=1)
    o_ref[...] = _layer_norm(y, g_ref[...], b_ref[...])


def _pool_ln(h, w_bf16, scale, g, b, tiles_per_batch):
    rows = h.shape[0]
    n_tiles = rows // TM
    halo_per_tile = TM // HALO

    def halo_map(i):
        batch_start = (i - 1) % tiles_per_batch == 0
        blk = jnp.where(batch_start, halo_per_tile - 1, i * halo_per_tile - 1)
        return (jnp.where(i == 0, 0, blk), 0)

    return pl.pallas_call(
        functools.partial(_pool_ln_kernel, tiles_per_batch),
        grid=(n_tiles,),
        in_specs=[
            pl.BlockSpec((TM, D_MODEL), lambda i: (i, 0)),
            pl.BlockSpec((HALO, D_MODEL), halo_map),
            _const_spec(w_bf16.shape),
            _const_spec(scale.shape),
            _const_spec(g.shape),
            _const_spec(b.shape),
        ],
        out_specs=pl.BlockSpec((TM, D_MODEL), lambda i: (i, 0)),
        out_shape=jax.ShapeDtypeStruct((rows, D_MODEL), jnp.float32),
        compiler_params=_params(1),
        name="pool_ln",
    )(h, h, w_bf16, scale, g, b)


def _ffn_kernel(tiles_per_batch, h_ref, win_ref, cw_ref, wout_ref, g_ref, b_ref, o_ref,
                act_ref, carry_ref, meta_tail_ref):
    i = pl.program_id(0)

    @pl.when(i == 0)
    def _():
        carry_ref[...] = jnp.zeros_like(carry_ref)

    h = h_ref[...]
    hb = h.astype(jnp.bfloat16)
    row = lax.broadcasted_iota(jnp.int32, (TM, 1), 0)
    valid = jnp.logical_or(i > 0, row >= META_ROW0)
    batch_start = jnp.logical_and(i > 0, (i - 1) % tiles_per_batch == 0)
    FC = FFN_CHUNK
    for c in range(N_FFN_CHUNKS):
        cols = slice(c * 2 * FC, (c + 1) * 2 * FC)
        u = jnp.dot(hb, win_ref[c], preferred_element_type=jnp.float32)
        u = jnp.where(valid, u, 0.0)
        prev = jnp.where(batch_start, meta_tail_ref[:, cols], carry_ref[:, cols])
        p1 = prev[SUBLANES - 1:SUBLANES, :]
        p2 = prev[SUBLANES - 2:SUBLANES - 1, :]
        u1 = jnp.where(row == 0, p1, pltpu.roll(u, 1, axis=0))
        u2 = jnp.where(row == 0, p2, jnp.where(row == 1, p1, pltpu.roll(u, 2, axis=0)))
        cw = cw_ref[c]
        cv = cw[3:4, :] + cw[0:1, :] * u2 + cw[1:2, :] * u1 + cw[2:3, :] * u
        tail = u[TM - SUBLANES:, :]
        carry_ref[:, cols] = tail

        @pl.when(i == 0)
        def _():
            meta_tail_ref[:, cols] = tail

        a = cv[:, :FC]
        gate = cv[:, FC:]
        act_ref[:, c * FC:(c + 1) * FC] = (a * jax.nn.sigmoid(a) * gate).astype(jnp.bfloat16)
    ffn = jnp.dot(act_ref[...], wout_ref[...], preferred_element_type=jnp.float32)
    o_ref[...] = _layer_norm(ALPHA * h + ffn, g_ref[...], b_ref[...])


def _ffn_ln(h, win_r, cw_r, wout, g, b, tiles_per_batch, drop_meta):
    rows = h.shape[0]
    n_tiles = rows // TM
    if drop_meta:
        out_rows, out_map = rows - TM, (lambda i: (jnp.maximum(i - 1, 0), 0))
    else:
        out_rows, out_map = rows, (lambda i: (i, 0))
    return pl.pallas_call(
        functools.partial(_ffn_kernel, tiles_per_batch),
        grid=(n_tiles,),
        in_specs=[
            pl.BlockSpec((TM, D_MODEL), lambda i: (i, 0)),
            _const_spec(win_r.shape),
            _const_spec(cw_r.shape),
            _const_spec(wout.shape),
            _const_spec(g.shape),
            _const_spec(b.shape),
        ],
        out_specs=pl.BlockSpec((TM, D_MODEL), out_map),
        out_shape=jax.ShapeDtypeStruct((out_rows, D_MODEL), jnp.float32),
        scratch_shapes=[
            pltpu.VMEM((TM, D_FF), jnp.bfloat16),
            pltpu.VMEM((SUBLANES, 2 * D_FF), jnp.float32),
            pltpu.VMEM((SUBLANES, 2 * D_FF), jnp.float32),
        ],
        compiler_params=_params(1),
        name="ffn_ln",
    )(h, win_r, cw_r, wout, g, b)


def _qkv_kernel(tiles_per_batch, h_ref, wq_ref, wkv_ref, wf_ref, bf_ref,
                q_ref, k_ref, v_ref, c_ref, carry_ref, meta_c_ref):
    i = pl.program_id(0)

    @pl.when(i == 0)
    def _():
        carry_ref[...] = jnp.zeros_like(carry_ref)

    hb = h_ref[...].astype(jnp.bfloat16)
    q = jnp.dot(hb, wq_ref[...], preferred_element_type=jnp.float32)
    q_ref[...] = (q * (HEAD_DIM ** -0.5)).astype(jnp.bfloat16)
    k_ref[...] = jnp.dot(hb, wkv_ref[:, :D_MODEL], preferred_element_type=jnp.float32).astype(jnp.bfloat16)
    v_ref[...] = jnp.dot(hb, wkv_ref[:, D_MODEL:], preferred_element_type=jnp.float32).astype(jnp.bfloat16)

    z = jnp.dot(hb, wf_ref[...], preferred_element_type=jnp.float32) + bf_ref[...]
    logf = jnp.minimum(z, 0.0) - jnp.log1p(jnp.exp(-jnp.abs(z)))
    row = lax.broadcasted_iota(jnp.int32, (TM, 1), 0)
    valid = jnp.logical_or(i > 0, row >= META_ROW0)
    x = jnp.where(valid, logf, 0.0)
    s = 1
    while s < TM:
        x = x + jnp.where(row >= s, pltpu.roll(x, s, axis=0), 0.0)
        s *= 2
    batch_start = jnp.logical_and(i > 0, (i - 1) % tiles_per_batch == 0)
    last = SUBLANES - 1
    base = jnp.where(batch_start, meta_c_ref[last:, :], carry_ref[last:, :])
    c = x + base
    c_ref[...] = c[:, :N_HEADS]
    tail = c[TM - SUBLANES:, :]
    carry_ref[...] = tail

    @pl.when(i == 0)
    def _():
        meta_c_ref[...] = tail


def _qkv(h, wq, wkv, wf_pad, bf_pad, tiles_per_batch):
    rows = h.shape[0]
    n_tiles = rows // TM
    tile = pl.BlockSpec((TM, D_MODEL), lambda i: (i, 0))
    bf16_out = jax.ShapeDtypeStruct((rows, D_MODEL), jnp.bfloat16)
    return pl.pallas_call(
        functools.partial(_qkv_kernel, tiles_per_batch),
        grid=(n_tiles,),
        in_specs=[tile, _const_spec(wq.shape), _const_spec(wkv.shape),
                  _const_spec(wf_pad.shape), _const_spec(bf_pad.shape)],
        out_specs=[tile, tile, tile, pl.BlockSpec((TM, N_HEADS), lambda i: (i, 0))],
        out_shape=[bf16_out, bf16_out, bf16_out,
                   jax.ShapeDtypeStruct((rows, N_HEADS), jnp.float32)],
        scratch_shapes=[pltpu.VMEM((SUBLANES, LANES), jnp.float32),
                        pltpu.VMEM((SUBLANES, LANES), jnp.float32)],
        compiler_params=_params(1),
        name="qkv_proj",
    )(h, wq, wkv, wf_pad, bf_pad)


def _kv_steps(f, tiles_per_batch):
    qi = jnp.maximum(f - 1, 0) % tiles_per_batch
    return jnp.where(f == 0, 1, qi + 2)


def _kv_block(f, j, tiles_per_batch):
    b = jnp.maximum(f - 1, 0) // tiles_per_batch
    jj = jnp.minimum(j, _kv_steps(f, tiles_per_batch) - 1)
    return jnp.where(jj == 0, 0, b * tiles_per_batch + jj)


def _attn_kernel(tiles_per_batch, q_ref, k_ref, v_ref, cq_ref, ckt_ref, o_ref, m_ref, l_ref, acc_ref):
    f = pl.program_id(0)
    j = pl.program_id(1)
    nkv = _kv_steps(f, tiles_per_batch)

    @pl.when(j == 0)
    def _():
        m_ref[...] = jnp.full_like(m_ref, NEG_INF)
        l_ref[...] = jnp.zeros_like(l_ref)
        acc_ref[...] = jnp.zeros_like(acc_ref)

    lane = lax.broadcasted_iota(jnp.int32, (1, LANES), 1)
    low_half = lane < HEAD_DIM

    def step(masked):
        if masked:
            qrow = lax.broadcasted_iota(jnp.int32, (TM, TM), 0)
            kcol = lax.broadcasted_iota(jnp.int32, (TM, TM), 1)
            kmin = jnp.where(j == 0, META_ROW0, 0)
            not_diag = j != nkv - 1
            keep = jnp.logical_and(kcol >= kmin, jnp.logical_or(kcol <= qrow, not_diag))
        for p in range(N_HEADS // 2):
            cols = slice(p * LANES, (p + 1) * LANES)
            qp = q_ref[:, cols]
            kp = k_ref[:, cols]
            vp = v_ref[:, cols]
            pv, alpha = [], []
            for hh in range(2):
                hd = 2 * p + hh
                in_head = low_half if hh == 0 else jnp.logical_not(low_half)
                qm = jnp.where(in_head, qp, jnp.zeros_like(qp))
                s = lax.dot_general(qm, kp, (((1,), (1,)), ((), ())),
                                    preferred_element_type=jnp.float32)
                s = s + cq_ref[:, hd:hd + 1] - ckt_ref[hd:hd + 1, :]
                if masked:
                    s = jnp.where(keep, s, NEG_INF)
                m_prev = m_ref[:, hd:hd + 1]
                m_new = jnp.maximum(m_prev, jnp.max(s, axis=1, keepdims=True))
                a = jnp.exp(m_prev - m_new)
                e = jnp.exp(s - m_new)
                l_ref[:, hd:hd + 1] = a * l_ref[:, hd:hd + 1] + jnp.sum(e, axis=1, keepdims=True)
                m_ref[:, hd:hd + 1] = m_new
                pv.append(jnp.dot(e.astype(jnp.bfloat16), vp, preferred_element_type=jnp.float32))
                alpha.append(a)
            acc_ref[:, cols] = (acc_ref[:, cols] * jnp.where(low_half, alpha[0], alpha[1])
                                + jnp.where(low_half, pv[0], pv[1]))

    edge = jnp.logical_or(j == 0, j == nkv - 1)

    @pl.when(jnp.logical_and(j < nkv, edge))
    def _():
        step(True)

    @pl.when(jnp.logical_and(j < nkv, jnp.logical_not(edge)))
    def _():
        step(False)

    @pl.when(j == nkv - 1)
    def _():
        for p in range(N_HEADS // 2):
            cols = slice(p * LANES, (p + 1) * LANES)
            inv = jnp.where(low_half, 1.0 / l_ref[:, 2 * p:2 * p + 1], 1.0 / l_ref[:, 2 * p + 1:2 * p + 2])
            o_ref[:, cols] = (acc_ref[:, cols] * inv).astype(o_ref.dtype)


def _attention(q, k, v, c, ct, tiles_per_batch):
    rows = q.shape[0]
    n_tiles = rows // TM
    kv_map = lambda f, j: (_kv_block(f, j, tiles_per_batch), 0)
    q_tile = pl.BlockSpec((TM, D_MODEL), lambda f, j: (f, 0))
    return pl.pallas_call(
        functools.partial(_attn_kernel, tiles_per_batch),
        grid=(n_tiles, tiles_per_batch + 1),
        in_specs=[
            q_tile,
            pl.BlockSpec((TM, D_MODEL), kv_map),
            pl.BlockSpec((TM, D_MODEL), kv_map),
            pl.BlockSpec((TM, N_HEADS), lambda f, j: (f, 0)),
            pl.BlockSpec((N_HEADS, TM), lambda f, j: (0, _kv_block(f, j, tiles_per_batch))),
        ],
        out_specs=q_tile,
        out_shape=jax.ShapeDtypeStruct((rows, D_MODEL), jnp.bfloat16),
        scratch_shapes=[
            pltpu.VMEM((TM, LANES), jnp.float32),
            pltpu.VMEM((TM, LANES), jnp.float32),
            pltpu.VMEM((TM, D_MODEL), jnp.float32),
        ],
        compiler_params=_params(2),
        name="fox_attention",
    )(q, k, v, c, ct)


def _oproj_ln_kernel(h_ref, o_ref, wo_ref, g_ref, b_ref, out_ref):
    mix = jnp.dot(o_ref[...], wo_ref[...], preferred_element_type=jnp.float32)
    out_ref[...] = _layer_norm(ALPHA * h_ref[...] + mix, g_ref[...], b_ref[...])


def _oproj_ln(h, o, wo, g, b):
    rows = h.shape[0]
    tile = pl.BlockSpec((TM, D_MODEL), lambda i: (i, 0))
    return pl.pallas_call(
        _oproj_ln_kernel,
        grid=(rows // TM,),
        in_specs=[tile, tile, _const_spec(wo.shape), _const_spec(g.shape), _const_spec(b.shape)],
        out_specs=tile,
        out_shape=jax.ShapeDtypeStruct((rows, D_MODEL), jnp.float32),
        compiler_params=_params(1),
        name="oproj_ln",
    )(h, o, wo, g, b)


def _ffn_weights(w_in, conv_w, conv_b, w_out):
    F, FC, NC = D_FF, FFN_CHUNK, N_FFN_CHUNKS

    def regroup(m):
        a = m[..., :F].reshape(m.shape[:-1] + (NC, FC))
        g = m[..., F:].reshape(m.shape[:-1] + (NC, FC))
        return jnp.moveaxis(jnp.concatenate([a, g], axis=-1), -2, 0)

    win_r = regroup(w_in).astype(jnp.bfloat16)
    taps = jnp.concatenate([conv_w, conv_b[None, :],
                            jnp.zeros((SUBLANES - CONV_WIDTH - 1, 2 * F), conv_w.dtype)], axis=0)
    cw_r = regroup(taps)
    return win_r, cw_r, w_out.astype(jnp.bfloat16)


def kernel(x, meta, pool_w, pool_scale, w_kv, w_f, b_f, w_q, w_o, ffn_w_in, ffn_conv_w, ffn_conv_b,
           ffn_w_out, ln_g, ln_b):
    B, S, D = x.shape
    assert D == D_MODEL and S % TM == 0 and meta.shape == (N_META, D_MODEL)
    assert pool_w.shape[0] == N_A_LAYERS == 1 and w_q.shape[0] == DEPTH - N_A_LAYERS == 1
    tpb = S // TM

    h = jnp.concatenate([jnp.zeros((META_ROW0, D), x.dtype), meta.astype(x.dtype), x.reshape(B * S, D)], axis=0)
    row2 = lambda a: a.reshape(1, -1)

    h = _pool_ln(h, pool_w[0].astype(jnp.bfloat16), row2(pool_scale[0]), row2(ln_g[0, 0]), row2(ln_b[0, 0]), tpb)
    h = _ffn_ln(h, *_ffn_weights(ffn_w_in[0], ffn_conv_w[0], ffn_conv_b[0], ffn_w_out[0]),
                row2(ln_g[0, 1]), row2(ln_b[0, 1]), tpb, drop_meta=False)

    wf_pad = jnp.pad(w_f, ((0, 0), (0, LANES - N_HEADS))).astype(jnp.bfloat16)
    bf_pad = jnp.pad(b_f, (0, LANES - N_HEADS)).reshape(1, LANES)
    q, k, v, c = _qkv(h, w_q[0].astype(jnp.bfloat16), w_kv.astype(jnp.bfloat16), wf_pad, bf_pad, tpb)
    o = _attention(q, k, v, c, c.T, tpb)
    h = _oproj_ln(h, o, w_o[0].astype(jnp.bfloat16), row2(ln_g[1, 0]), row2(ln_b[1, 0]))
    h = _ffn_ln(h, *_ffn_weights(ffn_w_in[1], ffn_conv_w[1], ffn_conv_b[1], ffn_w_out[1]),
                row2(ln_g[1, 1]), row2(ln_b[1, 1]), tpb, drop_meta=True)
    return h.reshape(B, S, D)
```

```python
import functools
import math

import numpy as np
import jax
import jax.numpy as jnp
from jax import lax
from jax.experimental import pallas as pl
from jax.experimental.pallas import tpu as pltpu

D_MODEL = 1024
DEPTH = 2
N_META = 16
POOL_WINDOWS = (2, 4, 8, 16)
N_POOL_GROUPS = len(POOL_WINDOWS)
POOL_GROUP_DIM = D_MODEL // N_POOL_GROUPS
HEAD_DIM = 64
N_HEADS = D_MODEL // HEAD_DIM
D_FF = ((8 * D_MODEL // 3 + 127) // 128) * 128
CONV_WIDTH = 3
N_A_LAYERS = DEPTH // 2
ALPHA = (2.0 * DEPTH) ** 0.25
LN_EPS = 1e-5
NEG_INF = -1e30
LOG2E = math.log2(math.e)

TM = 512
META_ROW0 = TM - N_META
HALO = 16
FFN_CHUNK = 256
N_FFN_CHUNKS = D_FF // FFN_CHUNK
LANES = 128
SUBLANES = 8
AUG = LANES // N_HEADS
VMEM_LIMIT_BYTES = 56 * 1024 * 1024

assert D_FF % FFN_CHUNK == 0 and AUG >= 6


def _layer_norm(y, g, b):
    mu = jnp.mean(y, axis=-1, keepdims=True)
    xc = y - mu
    var = jnp.mean(xc * xc, axis=-1, keepdims=True)
    return xc * lax.rsqrt(var + LN_EPS) * g + b


def _const_spec(shape):
    nd = len(shape)
    return pl.BlockSpec(shape, lambda *_: (0,) * nd, pipeline_mode=pl.Buffered(1))


def _params(n_axes):
    return pltpu.CompilerParams(dimension_semantics=("arbitrary",) * n_axes,
                                vmem_limit_bytes=VMEM_LIMIT_BYTES)


def _pool_ln_kernel(h_ref, halo_ref, w_ref, scale_ref, g_ref, b_ref, o_ref):
    i = pl.program_id(0)
    h = h_ref[...]
    halo = halo_ref[...] * (i > 0).astype(jnp.float32)
    row = lax.broadcasted_iota(jnp.int32, (TM, 1), 0)
    t = jnp.where(i == 0, row - (META_ROW0 - 1), HALO)
    outs = []
    G = POOL_GROUP_DIM
    for gi, w in enumerate(POOL_WINDOWS):
        sl = slice(gi * G, (gi + 1) * G)
        hg = h[:, sl]
        s = jnp.concatenate([halo[:, sl], hg], axis=0)
        k = 1
        while k < w:
            s = s + pltpu.roll(s, k, axis=0)
            k *= 2
        count = jnp.clip(t, 1, w).astype(jnp.float32)
        pooled = s[HALO:, :] / count
        diff = (pooled - hg).astype(jnp.bfloat16)
        mixed = jnp.dot(diff, w_ref[gi], preferred_element_type=jnp.float32)
        outs.append(ALPHA * hg + mixed * scale_ref[:, sl])
    y = jnp.concatenate(outs, axis=1)
    o_ref[...] = _layer_norm(y, g_ref[...], b_ref[...])


def _pool_ln(h, w_bf16, scale, g, b, tiles_per_batch):
    rows = h.shape[0]
    n_tiles = rows // TM
    halo_per_tile = TM // HALO

    def halo_map(i):
        batch_start = (i - 1) % tiles_per_batch == 0
        blk = jnp.where(batch_start, halo_per_tile - 1, i * halo_per_tile - 1)
        return (jnp.where(i == 0, 0, blk), 0)

    return pl.pallas_call(
        _pool_ln_kernel,
        grid=(n_tiles,),
        in_specs=[
            pl.BlockSpec((TM, D_MODEL), lambda i: (i, 0)),
            pl.BlockSpec((HALO, D_MODEL), halo_map),
            _const_spec(w_bf16.shape),
            _const_spec(scale.shape),
            _const_spec(g.shape),
            _const_spec(b.shape),
        ],
        out_specs=pl.BlockSpec((TM, D_MODEL), lambda i: (i, 0)),
        out_shape=jax.ShapeDtypeStruct((rows, D_MODEL), jnp.float32),
        compiler_params=_params(1),
        name="pool_ln",
    )(h, h, w_bf16, scale, g, b)


def _ffn_kernel(tiles_per_batch, h_ref, win_ref, cw_ref, wout_ref, g_ref, b_ref, o_ref,
                act_ref, carry_ref, meta_tail_ref):
    i = pl.program_id(0)

    @pl.when(i == 0)
    def _():
        carry_ref[...] = jnp.zeros_like(carry_ref)

    h = h_ref[...]
    hb = h.astype(jnp.bfloat16)
    row = lax.broadcasted_iota(jnp.int32, (TM, 1), 0)
    valid = jnp.logical_or(i > 0, row >= META_ROW0)
    batch_start = jnp.logical_and(i > 0, (i - 1) % tiles_per_batch == 0)
    FC = FFN_CHUNK
    for c in range(N_FFN_CHUNKS):
        cols = slice(c * 2 * FC, (c + 1) * 2 * FC)
        u = jnp.dot(hb, win_ref[c], preferred_element_type=jnp.float32)
        u = jnp.where(valid, u, 0.0)
        prev = jnp.where(batch_start, meta_tail_ref[:, cols], carry_ref[:, cols])
        p1 = prev[SUBLANES - 1:SUBLANES, :]
        p2 = prev[SUBLANES - 2:SUBLANES - 1, :]
        u1 = jnp.where(row == 0, p1, pltpu.roll(u, 1, axis=0))
        u2 = jnp.where(row == 0, p2, jnp.where(row == 1, p1, pltpu.roll(u, 2, axis=0)))
        cw = cw_ref[c]
        cv = cw[3:4, :] + cw[0:1, :] * u2 + cw[1:2, :] * u1 + cw[2:3, :] * u
        tail = u[TM - SUBLANES:, :]
        carry_ref[:, cols] = tail

        @pl.when(i == 0)
        def _():
            meta_tail_ref[:, cols] = tail

        a = cv[:, :FC]
        gate = cv[:, FC:]
        act_ref[:, c * FC:(c + 1) * FC] = (a * jax.nn.sigmoid(a) * gate).astype(jnp.bfloat16)
    ffn = jnp.dot(act_ref[...], wout_ref[...], preferred_element_type=jnp.float32)
    o_ref[...] = _layer_norm(ALPHA * h + ffn, g_ref[...], b_ref[...])


def _ffn_ln(h, win_r, cw_r, wout, g, b, tiles_per_batch, drop_meta):
    rows = h.shape[0]
    n_tiles = rows // TM
    if drop_meta:
        out_rows, out_map = rows - TM, (lambda i: (jnp.maximum(i - 1, 0), 0))
    else:
        out_rows, out_map = rows, (lambda i: (i, 0))
    return pl.pallas_call(
        functools.partial(_ffn_kernel, tiles_per_batch),
        grid=(n_tiles,),
        in_specs=[
            pl.BlockSpec((TM, D_MODEL), lambda i: (i, 0)),
            _const_spec(win_r.shape),
            _const_spec(cw_r.shape),
            _const_spec(wout.shape),
            _const_spec(g.shape),
            _const_spec(b.shape),
        ],
        out_specs=pl.BlockSpec((TM, D_MODEL), out_map),
        out_shape=jax.ShapeDtypeStruct((out_rows, D_MODEL), jnp.float32),
        scratch_shapes=[
            pltpu.VMEM((TM, D_FF), jnp.bfloat16),
            pltpu.VMEM((SUBLANES, 2 * D_FF), jnp.float32),
            pltpu.VMEM((SUBLANES, 2 * D_FF), jnp.float32),
        ],
        compiler_params=_params(1),
        name="ffn_ln",
    )(h, win_r, cw_r, wout, g, b)


def _qkv_kernel(tiles_per_batch, h_ref, wq_ref, wkv_ref, wf_ref, bf_ref,
                q_ref, k_ref, v_ref, qa_ref, ka_ref, carry_ref, meta_c_ref):
    i = pl.program_id(0)

    @pl.when(i == 0)
    def _():
        carry_ref[...] = jnp.zeros_like(carry_ref)

    hb = h_ref[...].astype(jnp.bfloat16)
    q = jnp.dot(hb, wq_ref[...], preferred_element_type=jnp.float32)
    q_ref[...] = (q * (HEAD_DIM ** -0.5 * LOG2E)).astype(jnp.bfloat16)
    k_ref[...] = jnp.dot(hb, wkv_ref[:, :D_MODEL], preferred_element_type=jnp.float32).astype(jnp.bfloat16)
    v_ref[...] = jnp.dot(hb, wkv_ref[:, D_MODEL:], preferred_element_type=jnp.float32).astype(jnp.bfloat16)

    z = jnp.dot(hb, wf_ref[...], preferred_element_type=jnp.float32) + bf_ref[...]
    logf = jnp.minimum(z, 0.0) - jnp.log1p(jnp.exp(-jnp.abs(z)))
    row = lax.broadcasted_iota(jnp.int32, (TM, 1), 0)
    valid = jnp.logical_or(i > 0, row >= META_ROW0)
    x = jnp.where(valid, logf, 0.0)
    s = 1
    while s < TM:
        x = x + jnp.where(row >= s, pltpu.roll(x, s, axis=0), 0.0)
        s *= 2
    batch_start = jnp.logical_and(i > 0, (i - 1) % tiles_per_batch == 0)
    last = SUBLANES - 1
    base = jnp.where(batch_start, meta_c_ref[last:, :], carry_ref[last:, :])
    c = x + base
    tail = c[TM - SUBLANES:, :]
    carry_ref[...] = tail

    @pl.when(i == 0)
    def _():
        meta_c_ref[...] = tail

    c2 = c * LOG2E
    hi = c2.astype(jnp.bfloat16)
    r1 = c2 - hi.astype(jnp.float32)
    mid = r1.astype(jnp.bfloat16)
    lo = (r1 - mid.astype(jnp.float32)).astype(jnp.bfloat16)
    src = lax.broadcasted_iota(jnp.int32, (LANES, LANES), 0)
    dst = lax.broadcasted_iota(jnp.int32, (LANES, LANES), 1)

    def spread(piece, offset):
        sel = jnp.logical_and(dst == AUG * src + offset, src < N_HEADS).astype(jnp.bfloat16)
        return jnp.dot(piece, sel, preferred_element_type=jnp.float32)

    slot = lax.broadcasted_iota(jnp.int32, (1, LANES), 1) % AUG
    ones_q = jnp.logical_and(slot >= 3, slot < 6).astype(jnp.float32)
    ones_k = (slot < 3).astype(jnp.float32)
    qa_ref[...] = (spread(hi, 0) + spread(mid, 1) + spread(lo, 2) + ones_q).astype(jnp.bfloat16)
    ka_ref[...] = (ones_k - (spread(hi, 3) + spread(mid, 4) + spread(lo, 5))).astype(jnp.bfloat16)


def _qkv(h, wq, wkv, wf_pad, bf_pad, tiles_per_batch):
    rows = h.shape[0]
    n_tiles = rows // TM
    tile = pl.BlockSpec((TM, D_MODEL), lambda i: (i, 0))
    aug_tile = pl.BlockSpec((TM, LANES), lambda i: (i, 0))
    wide = jax.ShapeDtypeStruct((rows, D_MODEL), jnp.bfloat16)
    narrow = jax.ShapeDtypeStruct((rows, LANES), jnp.bfloat16)
    return pl.pallas_call(
        functools.partial(_qkv_kernel, tiles_per_batch),
        grid=(n_tiles,),
        in_specs=[tile, _const_spec(wq.shape), _const_spec(wkv.shape),
                  _const_spec(wf_pad.shape), _const_spec(bf_pad.shape)],
        out_specs=[tile, tile, tile, aug_tile, aug_tile],
        out_shape=[wide, wide, wide, narrow, narrow],
        scratch_shapes=[pltpu.VMEM((SUBLANES, LANES), jnp.float32),
                        pltpu.VMEM((SUBLANES, LANES), jnp.float32)],
        compiler_params=_params(1),
        name="qkv_proj",
    )(h, wq, wkv, wf_pad, bf_pad)


def _attention_steps(n_tiles, tiles_per_batch):
    f_tab, kv_tab, first, last, kmin, diag = [], [], [], [], [], []
    for f in range(n_tiles):
        if f == 0:
            kvs = [0]
        else:
            b, qi = divmod(f - 1, tiles_per_batch)
            kvs = [0] + [1 + b * tiles_per_batch + t for t in range(qi + 1)]
        for n, kv in enumerate(kvs):
            f_tab.append(f)
            kv_tab.append(kv)
            first.append(int(n == 0))
            last.append(int(n == len(kvs) - 1))
            is_diag = kv == f
            diag.append(int(is_diag))
            kmin.append(META_ROW0 if kv == 0 else (0 if is_diag else -1))
    return [np.asarray(t, np.int32) for t in (f_tab, kv_tab, first, last, kmin, diag)]


def _attn_kernel(f_tab, kv_tab, first_tab, last_tab, kmin_tab, diag_tab,
                 q_ref, qa_ref, k_ref, ka_ref, v_ref, o_ref, m_ref, acc_ref):
    del f_tab, kv_tab
    step_id = pl.program_id(0)
    kmin = kmin_tab[step_id]

    @pl.when(first_tab[step_id] == 1)
    def _():
        m_ref[...] = jnp.full_like(m_ref, NEG_INF)
        acc_ref[...] = jnp.zeros_like(acc_ref)

    lane = lax.broadcasted_iota(jnp.int32, (1, LANES), 1)
    low_half = lane < HEAD_DIM
    nt_dims = (((1,), (1,)), ((), ()))
    n_kblk = TM // LANES

    def step(masked):
        if masked:
            qrow = lax.broadcasted_iota(jnp.int32, (TM, TM), 0)
            kcol = lax.broadcasted_iota(jnp.int32, (TM, TM), 1)
            causal = diag_tab[step_id] == 1
            keep = jnp.logical_and(kcol >= kmin, jnp.logical_or(kcol <= qrow, jnp.logical_not(causal)))
        qa = qa_ref[...]
        ka = ka_ref[...]
        for p in range(N_HEADS // 2):
            cols = slice(p * LANES, (p + 1) * LANES)
            qp = q_ref[:, cols]
            k_ext = jnp.concatenate([k_ref[:, cols], ka], axis=1)
            vp = v_ref[:, cols]
            for hh in range(2):
                hd = 2 * p + hh
                in_head = low_half if hh == 0 else jnp.logical_not(low_half)
                ones_lane = HEAD_DIM if hh == 0 else 0
                aug_lanes = jnp.logical_and(lane >= AUG * hd, lane < AUG * (hd + 1))
                q_ext = jnp.concatenate([jnp.where(in_head, qp, jnp.zeros_like(qp)),
                                         jnp.where(aug_lanes, qa, jnp.zeros_like(qa))], axis=1)
                vm = jnp.where(in_head, vp, jnp.where(lane == ones_lane, 1.0, 0.0).astype(vp.dtype))
                s = lax.dot_general(q_ext, k_ext, nt_dims, preferred_element_type=jnp.float32)
                if masked:
                    s = jnp.where(keep, s, NEG_INF)
                m_prev = m_ref[hd]
                m_part = s[:, :LANES]
                for kb in range(1, n_kblk):
                    m_part = jnp.maximum(m_part, s[:, kb * LANES:(kb + 1) * LANES])
                m_new = jnp.maximum(m_prev, jnp.max(m_part, axis=1, keepdims=True))
                alpha = jnp.exp2(m_prev - m_new)
                pe = jnp.concatenate([jnp.exp2(s[:, kb * LANES:(kb + 1) * LANES] - m_new)
                                      for kb in range(n_kblk)], axis=1).astype(jnp.bfloat16)
                pv = jnp.dot(pe, vm, preferred_element_type=jnp.float32)
                acc_ref[hd] = alpha * acc_ref[hd] + pv
                m_ref[hd] = m_new

    @pl.when(kmin >= 0)
    def _():
        step(True)

    @pl.when(kmin < 0)
    def _():
        step(False)

    @pl.when(last_tab[step_id] == 1)
    def _():
        for p in range(N_HEADS // 2):
            cols = slice(p * LANES, (p + 1) * LANES)
            a0 = acc_ref[2 * p]
            a1 = acc_ref[2 * p + 1]
            l0 = a0[:, HEAD_DIM:HEAD_DIM + 1]
            l1 = a1[:, 0:1]
            o_ref[:, cols] = jnp.where(low_half, a0 / l0, a1 / l1).astype(o_ref.dtype)


def _attention(q, qa, k, ka, v, tiles_per_batch):
    rows = q.shape[0]
    tabs = _attention_steps(rows // TM, tiles_per_batch)
    q_map = lambda s, f_tab, kv_tab, *_: (f_tab[s], 0)
    kv_map = lambda s, f_tab, kv_tab, *_: (kv_tab[s], 0)
    grid_spec = pltpu.PrefetchScalarGridSpec(
        num_scalar_prefetch=len(tabs),
        grid=(len(tabs[0]),),
        in_specs=[
            pl.BlockSpec((TM, D_MODEL), q_map),
            pl.BlockSpec((TM, LANES), q_map),
            pl.BlockSpec((TM, D_MODEL), kv_map),
            pl.BlockSpec((TM, LANES), kv_map),
            pl.BlockSpec((TM, D_MODEL), kv_map),
        ],
        out_specs=pl.BlockSpec((TM, D_MODEL), q_map),
        scratch_shapes=[
            pltpu.VMEM((N_HEADS, TM, LANES), jnp.float32),
            pltpu.VMEM((N_HEADS, TM, LANES), jnp.float32),
        ],
    )
    return pl.pallas_call(
        _attn_kernel,
        grid_spec=grid_spec,
        out_shape=jax.ShapeDtypeStruct((rows, D_MODEL), jnp.bfloat16),
        compiler_params=_params(1),
        name="fox_attention",
    )(*[jnp.asarray(t) for t in tabs], q, qa, k, ka, v)


def _oproj_ln_kernel(h_ref, o_ref, wo_ref, g_ref, b_ref, out_ref):
    mix = jnp.dot(o_ref[...], wo_ref[...], preferred_element_type=jnp.float32)
    out_ref[...] = _layer_norm(ALPHA * h_ref[...] + mix, g_ref[...], b_ref[...])


def _oproj_ln(h, o, wo, g, b):
    rows = h.shape[0]
    tile = pl.BlockSpec((TM, D_MODEL), lambda i: (i, 0))
    return pl.pallas_call(
        _oproj_ln_kernel,
        grid=(rows // TM,),
        in_specs=[tile, tile, _const_spec(wo.shape), _const_spec(g.shape), _const_spec(b.shape)],
        out_specs=tile,
        out_shape=jax.ShapeDtypeStruct((rows, D_MODEL), jnp.float32),
        compiler_params=_params(1),
        name="oproj_ln",
    )(h, o, wo, g, b)


def _ffn_weights(w_in, conv_w, conv_b, w_out):
    F, FC, NC = D_FF, FFN_CHUNK, N_FFN_CHUNKS

    def regroup(m):
        a = m[..., :F].reshape(m.shape[:-1] + (NC, FC))
        g = m[..., F:].reshape(m.shape[:-1] + (NC, FC))
        return jnp.moveaxis(jnp.concatenate([a, g], axis=-1), -2, 0)

    win_r = regroup(w_in).astype(jnp.bfloat16)
    taps = jnp.concatenate([conv_w, conv_b[None, :],
                            jnp.zeros((SUBLANES - CONV_WIDTH - 1, 2 * F), conv_w.dtype)], axis=0)
    cw_r = regroup(taps)
    return win_r, cw_r, w_out.astype(jnp.bfloat16)


def kernel(x, meta, pool_w, pool_scale, w_kv, w_f, b_f, w_q, w_o, ffn_w_in, ffn_conv_w, ffn_conv_b,
           ffn_w_out, ln_g, ln_b):
    B, S, D = x.shape
    assert D == D_MODEL and S % TM == 0 and meta.shape == (N_META, D_MODEL)
    assert pool_w.shape[0] == N_A_LAYERS == 1 and w_q.shape[0] == DEPTH - N_A_LAYERS == 1
    tpb = S // TM

    h = jnp.concatenate([jnp.zeros((META_ROW0, D), x.dtype), meta.astype(x.dtype), x.reshape(B * S, D)], axis=0)
    row2 = lambda a: a.reshape(1, -1)

    h = _pool_ln(h, pool_w[0].astype(jnp.bfloat16), row2(pool_scale[0]), row2(ln_g[0, 0]), row2(ln_b[0, 0]), tpb)
    h = _ffn_ln(h, *_ffn_weights(ffn_w_in[0], ffn_conv_w[0], ffn_conv_b[0], ffn_w_out[0]),
                row2(ln_g[0, 1]), row2(ln_b[0, 1]), tpb, drop_meta=False)

    wf_pad = jnp.pad(w_f, ((0, 0), (0, LANES - N_HEADS))).astype(jnp.bfloat16)
    bf_pad = jnp.pad(b_f, (0, LANES - N_HEADS)).reshape(1, LANES)
    q, k, v, qa, ka = _qkv(h, w_q[0].astype(jnp.bfloat16), w_kv.astype(jnp.bfloat16), wf_pad, bf_pad, tpb)
    o = _attention(q, qa, k, ka, v, tpb)
    h = _oproj_ln(h, o, w_o[0].astype(jnp.bfloat16), row2(ln_g[1, 0]), row2(ln_b[1, 0]))
    h = _ffn_ln(h, *_ffn_weights(ffn_w_in[1], ffn_conv_w[1], ffn_conv_b[1], ffn_w_out[1]),
                row2(ln_g[1, 1]), row2(ln_b[1, 1]), tpb, drop_meta=True)
    return h.reshape(B, S, D)
```

```python
import functools
import math

import numpy as np
import jax
import jax.numpy as jnp
from jax import lax
from jax.experimental import pallas as pl
from jax.experimental.pallas import tpu as pltpu

D_MODEL = 1024
DEPTH = 2
N_META = 16
POOL_WINDOWS = (2, 4, 8, 16)
N_POOL_GROUPS = len(POOL_WINDOWS)
POOL_GROUP_DIM = D_MODEL // N_POOL_GROUPS
HEAD_DIM = 64
N_HEADS = D_MODEL // HEAD_DIM
D_FF = ((8 * D_MODEL // 3 + 127) // 128) * 128
CONV_WIDTH = 3
N_A_LAYERS = DEPTH // 2
ALPHA = (2.0 * DEPTH) ** 0.25
LN_EPS = 1e-5
NEG_INF = -1e30
LOG2E = math.log2(math.e)

TM = 512
META_ROW0 = TM - N_META
HALO = 16
FFN_CHUNK = 256
N_FFN_CHUNKS = D_FF // FFN_CHUNK
LANES = 128
SUBLANES = 8
AUG = LANES // N_HEADS
VMEM_LIMIT_BYTES = 56 * 1024 * 1024

assert D_FF % FFN_CHUNK == 0 and AUG >= 6


def _layer_norm(y, g, b):
    mu = jnp.mean(y, axis=-1, keepdims=True)
    xc = y - mu
    var = jnp.mean(xc * xc, axis=-1, keepdims=True)
    return xc * lax.rsqrt(var + LN_EPS) * g + b


def _const_spec(shape):
    nd = len(shape)
    return pl.BlockSpec(shape, lambda *_: (0,) * nd, pipeline_mode=pl.Buffered(1))


def _params(n_axes):
    return pltpu.CompilerParams(dimension_semantics=("arbitrary",) * n_axes,
                                vmem_limit_bytes=VMEM_LIMIT_BYTES)


def _pool_ln_kernel(tiles_per_batch, x_ref, xhalo_ref, meta_ref, w_ref, scale_ref, g_ref, b_ref, o_ref):
    i = pl.program_id(0)
    is_meta = i == 0
    meta_tile = meta_ref[...]
    h = jnp.where(is_meta, meta_tile, x_ref[...])
    batch_start = (i - 1) % tiles_per_batch == 0
    halo = jnp.where(batch_start, meta_tile[TM - HALO:, :], xhalo_ref[...])
    halo = jnp.where(is_meta, 0.0, halo)
    row = lax.broadcasted_iota(jnp.int32, (TM, 1), 0)
    t = jnp.where(is_meta, row - (META_ROW0 - 1), HALO)
    outs = []
    G = POOL_GROUP_DIM
    for gi, w in enumerate(POOL_WINDOWS):
        sl = slice(gi * G, (gi + 1) * G)
        hg = h[:, sl]
        s = jnp.concatenate([halo[:, sl], hg], axis=0)
        k = 1
        while k < w:
            s = s + pltpu.roll(s, k, axis=0)
            k *= 2
        count = jnp.clip(t, 1, w).astype(jnp.float32)
        pooled = s[HALO:, :] / count
        diff = (pooled - hg).astype(jnp.bfloat16)
        mixed = jnp.dot(diff, w_ref[gi], preferred_element_type=jnp.float32)
        outs.append(ALPHA * hg + mixed * scale_ref[:, sl])
    y = jnp.concatenate(outs, axis=1)
    o_ref[...] = _layer_norm(y, g_ref[...], b_ref[...])


def _pool_ln(x_flat, meta_tile, w_bf16, scale, g, b, tiles_per_batch):
    n_tiles = x_flat.shape[0] // TM + 1
    rows = n_tiles * TM
    halo_per_tile = TM // HALO

    def halo_map(i):
        blk = jnp.maximum((i - 1) * halo_per_tile - 1, 0)
        return (jnp.where((i - 1) % tiles_per_batch == 0, 0, blk), 0)

    return pl.pallas_call(
        functools.partial(_pool_ln_kernel, tiles_per_batch),
        grid=(n_tiles,),
        in_specs=[
            pl.BlockSpec((TM, D_MODEL), lambda i: (jnp.maximum(i - 1, 0), 0)),
            pl.BlockSpec((HALO, D_MODEL), halo_map),
            _const_spec(meta_tile.shape),
            _const_spec(w_bf16.shape),
            _const_spec(scale.shape),
            _const_spec(g.shape),
            _const_spec(b.shape),
        ],
        out_specs=pl.BlockSpec((TM, D_MODEL), lambda i: (i, 0)),
        out_shape=jax.ShapeDtypeStruct((rows, D_MODEL), jnp.float32),
        compiler_params=_params(1),
        name="pool_ln",
    )(x_flat, x_flat, meta_tile, w_bf16, scale, g, b)


def _ffn_kernel(tiles_per_batch, h_ref, win_ref, cw_ref, wout_ref, g_ref, b_ref, o_ref,
                act_ref, carry_ref, meta_tail_ref):
    i = pl.program_id(0)

    @pl.when(i == 0)
    def _():
        carry_ref[...] = jnp.zeros_like(carry_ref)
        meta_tail_ref[...] = jnp.zeros_like(meta_tail_ref)

    h = h_ref[...]
    row = lax.broadcasted_iota(jnp.int32, (TM, 1), 0)
    valid = jnp.logical_or(i > 0, row >= META_ROW0)
    hb = jnp.where(valid, h, 0.0).astype(jnp.bfloat16)
    batch_start = jnp.logical_and(i > 0, (i - 1) % tiles_per_batch == 0)
    FC = FFN_CHUNK
    for c in range(N_FFN_CHUNKS):
        cols = slice(c * 2 * FC, (c + 1) * 2 * FC)
        u = jnp.dot(hb, win_ref[c], preferred_element_type=jnp.float32)
        prev = jnp.where(batch_start, meta_tail_ref[:, cols], carry_ref[:, cols])
        p1 = prev[SUBLANES - 1:SUBLANES, :]
        p2 = prev[SUBLANES - 2:SUBLANES - 1, :]
        u1 = jnp.where(row == 0, p1, pltpu.roll(u, 1, axis=0))
        u2 = jnp.where(row == 0, p2, jnp.where(row == 1, p1, pltpu.roll(u, 2, axis=0)))
        cw = cw_ref[c]
        cv = cw[3:4, :] + cw[0:1, :] * u2 + cw[1:2, :] * u1 + cw[2:3, :] * u
        tail = u[TM - SUBLANES:, :]
        carry_ref[:, cols] = tail
        meta_tail_ref[:, cols] = jnp.where(i == 0, tail, meta_tail_ref[:, cols])

        a = cv[:, :FC]
        gate = cv[:, FC:]
        act_ref[:, c * FC:(c + 1) * FC] = (a * jax.nn.sigmoid(a) * gate).astype(jnp.bfloat16)
    ffn = jnp.dot(act_ref[...], wout_ref[...], preferred_element_type=jnp.float32)
    o_ref[...] = _layer_norm(ALPHA * h + ffn, g_ref[...], b_ref[...])


def _ffn_ln(h, win_r, cw_r, wout, g, b, tiles_per_batch, drop_meta):
    rows = h.shape[0]
    n_tiles = rows // TM
    if drop_meta:
        out_rows, out_map = rows - TM, (lambda i: (jnp.maximum(i - 1, 0), 0))
    else:
        out_rows, out_map = rows, (lambda i: (i, 0))
    return pl.pallas_call(
        functools.partial(_ffn_kernel, tiles_per_batch),
        grid=(n_tiles,),
        in_specs=[
            pl.BlockSpec((TM, D_MODEL), lambda i: (i, 0)),
            _const_spec(win_r.shape),
            _const_spec(cw_r.shape),
            _const_spec(wout.shape),
            _const_spec(g.shape),
            _const_spec(b.shape),
        ],
        out_specs=pl.BlockSpec((TM, D_MODEL), out_map),
        out_shape=jax.ShapeDtypeStruct((out_rows, D_MODEL), jnp.float32),
        scratch_shapes=[
            pltpu.VMEM((TM, D_FF), jnp.bfloat16),
            pltpu.VMEM((SUBLANES, 2 * D_FF), jnp.float32),
            pltpu.VMEM((SUBLANES, 2 * D_FF), jnp.float32),
        ],
        compiler_params=_params(1),
        name="ffn_ln",
    )(h, win_r, cw_r, wout, g, b)


def _qkv_kernel(tiles_per_batch, h_ref, wq_ref, wkv_ref, wf_ref, bf_ref,
                q_ref, k_ref, v_ref, qa_ref, ka_ref, carry_ref, meta_c_ref):
    i = pl.program_id(0)

    @pl.when(i == 0)
    def _():
        carry_ref[...] = jnp.zeros_like(carry_ref)

    hb = h_ref[...].astype(jnp.bfloat16)
    q = jnp.dot(hb, wq_ref[...], preferred_element_type=jnp.float32)
    q_ref[...] = (q * (HEAD_DIM ** -0.5 * LOG2E)).astype(jnp.bfloat16)
    k_ref[...] = jnp.dot(hb, wkv_ref[:, :D_MODEL], preferred_element_type=jnp.float32).astype(jnp.bfloat16)
    v_ref[...] = jnp.dot(hb, wkv_ref[:, D_MODEL:], preferred_element_type=jnp.float32).astype(jnp.bfloat16)

    z = jnp.dot(hb, wf_ref[...], preferred_element_type=jnp.float32) + bf_ref[...]
    logf = jnp.minimum(z, 0.0) - jnp.log1p(jnp.exp(-jnp.abs(z)))
    row = lax.broadcasted_iota(jnp.int32, (TM, 1), 0)
    valid = jnp.logical_or(i > 0, row >= META_ROW0)
    x = jnp.where(valid, logf, 0.0)
    s = 1
    while s < TM:
        x = x + jnp.where(row >= s, pltpu.roll(x, s, axis=0), 0.0)
        s *= 2
    batch_start = jnp.logical_and(i > 0, (i - 1) % tiles_per_batch == 0)
    last = SUBLANES - 1
    base = jnp.where(batch_start, meta_c_ref[last:, :], carry_ref[last:, :])
    c = x + base
    tail = c[TM - SUBLANES:, :]
    carry_ref[...] = tail

    @pl.when(i == 0)
    def _():
        meta_c_ref[...] = tail

    c2 = c * LOG2E
    hi = c2.astype(jnp.bfloat16)
    r1 = c2 - hi.astype(jnp.float32)
    mid = r1.astype(jnp.bfloat16)
    lo = (r1 - mid.astype(jnp.float32)).astype(jnp.bfloat16)
    src = lax.broadcasted_iota(jnp.int32, (LANES, LANES), 0)
    dst = lax.broadcasted_iota(jnp.int32, (LANES, LANES), 1)

    def spread(piece, offset):
        sel = jnp.logical_and(dst == AUG * src + offset, src < N_HEADS).astype(jnp.bfloat16)
        return jnp.dot(piece, sel, preferred_element_type=jnp.float32)

    slot = lax.broadcasted_iota(jnp.int32, (1, LANES), 1) % AUG
    ones_q = jnp.logical_and(slot >= 3, slot < 6).astype(jnp.float32)
    ones_k = (slot < 3).astype(jnp.float32)
    qa_ref[...] = (spread(hi, 0) + spread(mid, 1) + spread(lo, 2) + ones_q).astype(jnp.bfloat16)
    ka_ref[...] = (ones_k - (spread(hi, 3) + spread(mid, 4) + spread(lo, 5))).astype(jnp.bfloat16)


def _qkv(h, wq, wkv, wf_pad, bf_pad, tiles_per_batch):
    rows = h.shape[0]
    n_tiles = rows // TM
    tile = pl.BlockSpec((TM, D_MODEL), lambda i: (i, 0))
    aug_tile = pl.BlockSpec((TM, LANES), lambda i: (i, 0))
    wide = jax.ShapeDtypeStruct((rows, D_MODEL), jnp.bfloat16)
    narrow = jax.ShapeDtypeStruct((rows, LANES), jnp.bfloat16)
    return pl.pallas_call(
        functools.partial(_qkv_kernel, tiles_per_batch),
        grid=(n_tiles,),
        in_specs=[tile, _const_spec(wq.shape), _const_spec(wkv.shape),
                  _const_spec(wf_pad.shape), _const_spec(bf_pad.shape)],
        out_specs=[tile, tile, tile, aug_tile, aug_tile],
        out_shape=[wide, wide, wide, narrow, narrow],
        scratch_shapes=[pltpu.VMEM((SUBLANES, LANES), jnp.float32),
                        pltpu.VMEM((SUBLANES, LANES), jnp.float32)],
        compiler_params=_params(1),
        name="qkv_proj",
    )(h, wq, wkv, wf_pad, bf_pad)


def _attention_steps(n_tiles, tiles_per_batch):
    f_tab, kv_tab, first, last, kmin, diag = [], [], [], [], [], []
    for f in range(n_tiles):
        if f == 0:
            kvs = [0]
        else:
            b, qi = divmod(f - 1, tiles_per_batch)
            kvs = [0] + [1 + b * tiles_per_batch + t for t in range(qi + 1)]
        for n, kv in enumerate(kvs):
            f_tab.append(f)
            kv_tab.append(kv)
            first.append(int(n == 0))
            last.append(int(n == len(kvs) - 1))
            is_diag = kv == f
            diag.append(int(is_diag))
            kmin.append(META_ROW0 if kv == 0 else (0 if is_diag else -1))
    return [np.asarray(t, np.int32) for t in (f_tab, kv_tab, first, last, kmin, diag)]


def _attn_kernel(f_tab, kv_tab, first_tab, last_tab, kmin_tab, diag_tab,
                 q_ref, qa_ref, k_ref, ka_ref, v_ref, o_ref, m_ref, acc_ref):
    del f_tab, kv_tab
    step_id = pl.program_id(0)
    kmin = kmin_tab[step_id]

    @pl.when(first_tab[step_id] == 1)
    def _():
        m_ref[...] = jnp.full_like(m_ref, NEG_INF)
        acc_ref[...] = jnp.zeros_like(acc_ref)

    lane = lax.broadcasted_iota(jnp.int32, (1, LANES), 1)
    low_half = lane < HEAD_DIM
    nt_dims = (((1,), (1,)), ((), ()))

    def step(masked, k0):
        tk = TM - k0
        n_kblk = tk // LANES
        if masked:
            qrow = lax.broadcasted_iota(jnp.int32, (TM, tk), 0)
            krow = lax.broadcasted_iota(jnp.int32, (TM, tk), 1) + k0
            causal = diag_tab[step_id] == 1
            keep = jnp.logical_and(krow >= kmin, jnp.logical_or(krow <= qrow, jnp.logical_not(causal)))
        qa = qa_ref[...]
        ka = ka_ref[k0:, :]
        n_pairs = N_HEADS // 2

        def pair_scores(p):
            cols = slice(p * LANES, (p + 1) * LANES)
            qp = q_ref[:, cols]
            k_ext = jnp.concatenate([k_ref[k0:, cols], ka], axis=1)
            vp = v_ref[k0:, cols]
            scores, vms = [], []
            for hh in range(2):
                hd = 2 * p + hh
                in_head = low_half if hh == 0 else jnp.logical_not(low_half)
                ones_lane = HEAD_DIM if hh == 0 else 0
                aug_lanes = jnp.logical_and(lane >= AUG * hd, lane < AUG * (hd + 1))
                q_ext = jnp.concatenate([jnp.where(in_head, qp, jnp.zeros_like(qp)),
                                         jnp.where(aug_lanes, qa, jnp.zeros_like(qa))], axis=1)
                vms.append(jnp.where(in_head, vp, jnp.where(lane == ones_lane, 1.0, 0.0).astype(vp.dtype)))
                scores.append(lax.dot_general(q_ext, k_ext, nt_dims, preferred_element_type=jnp.float32))
            return scores, vms

        def softmax_pv(hd, s, vm):
            if masked:
                s = jnp.where(keep, s, NEG_INF)
            m_prev = m_ref[hd]
            m_part = s[:, :LANES]
            for kb in range(1, n_kblk):
                m_part = jnp.maximum(m_part, s[:, kb * LANES:(kb + 1) * LANES])
            m_new = jnp.maximum(m_prev, jnp.max(m_part, axis=1, keepdims=True))
            alpha = jnp.exp2(m_prev - m_new)
            pe = jnp.concatenate([jnp.exp2(s[:, kb * LANES:(kb + 1) * LANES] - m_new)
                                  for kb in range(n_kblk)], axis=1).astype(jnp.bfloat16)
            pv = jnp.dot(pe, vm, preferred_element_type=jnp.float32)
            acc_ref[hd] = alpha * acc_ref[hd] + pv
            m_ref[hd] = m_new

        cur = pair_scores(0)
        for p in range(n_pairs):
            nxt = pair_scores(p + 1) if p + 1 < n_pairs else None
            for hh in range(2):
                softmax_pv(2 * p + hh, cur[0][hh], cur[1][hh])
            cur = nxt

    @pl.when(kmin > 0)
    def _():
        step(True, TM - LANES)

    @pl.when(kmin == 0)
    def _():
        step(True, 0)

    @pl.when(kmin < 0)
    def _():
        step(False, 0)

    @pl.when(last_tab[step_id] == 1)
    def _():
        for p in range(N_HEADS // 2):
            cols = slice(p * LANES, (p + 1) * LANES)
            a0 = acc_ref[2 * p]
            a1 = acc_ref[2 * p + 1]
            l0 = a0[:, HEAD_DIM:HEAD_DIM + 1]
            l1 = a1[:, 0:1]
            o_ref[:, cols] = jnp.where(low_half, a0 / l0, a1 / l1).astype(o_ref.dtype)


def _attention(q, qa, k, ka, v, tiles_per_batch):
    rows = q.shape[0]
    tabs = _attention_steps(rows // TM, tiles_per_batch)
    q_map = lambda s, f_tab, kv_tab, *_: (f_tab[s], 0)
    kv_map = lambda s, f_tab, kv_tab, *_: (kv_tab[s], 0)
    grid_spec = pltpu.PrefetchScalarGridSpec(
        num_scalar_prefetch=len(tabs),
        grid=(len(tabs[0]),),
        in_specs=[
            pl.BlockSpec((TM, D_MODEL), q_map),
            pl.BlockSpec((TM, LANES), q_map),
            pl.BlockSpec((TM, D_MODEL), kv_map),
            pl.BlockSpec((TM, LANES), kv_map),
            pl.BlockSpec((TM, D_MODEL), kv_map),
        ],
        out_specs=pl.BlockSpec((TM, D_MODEL), q_map),
        scratch_shapes=[
            pltpu.VMEM((N_HEADS, TM, LANES), jnp.float32),
            pltpu.VMEM((N_HEADS, TM, LANES), jnp.float32),
        ],
    )
    return pl.pallas_call(
        _attn_kernel,
        grid_spec=grid_spec,
        out_shape=jax.ShapeDtypeStruct((rows, D_MODEL), jnp.bfloat16),
        compiler_params=_params(1),
        name="fox_attention",
    )(*[jnp.asarray(t) for t in tabs], q, qa, k, ka, v)


def _oproj_ln_kernel(h_ref, o_ref, wo_ref, g_ref, b_ref, out_ref):
    mix = jnp.dot(o_ref[...], wo_ref[...], preferred_element_type=jnp.float32)
    out_ref[...] = _layer_norm(ALPHA * h_ref[...] + mix, g_ref[...], b_ref[...])


def _oproj_ln(h, o, wo, g, b):
    rows = h.shape[0]
    tile = pl.BlockSpec((TM, D_MODEL), lambda i: (i, 0))
    return pl.pallas_call(
        _oproj_ln_kernel,
        grid=(rows // TM,),
        in_specs=[tile, tile, _const_spec(wo.shape), _const_spec(g.shape), _const_spec(b.shape)],
        out_specs=tile,
        out_shape=jax.ShapeDtypeStruct((rows, D_MODEL), jnp.float32),
        compiler_params=_params(1),
        name="oproj_ln",
    )(h, o, wo, g, b)


def _ffn_weights(w_in, conv_w, conv_b, w_out):
    F, FC, NC = D_FF, FFN_CHUNK, N_FFN_CHUNKS

    def regroup(m):
        a = m[..., :F].reshape(m.shape[:-1] + (NC, FC))
        g = m[..., F:].reshape(m.shape[:-1] + (NC, FC))
        return jnp.moveaxis(jnp.concatenate([a, g], axis=-1), -2, 0)

    win_r = regroup(w_in).astype(jnp.bfloat16)
    taps = jnp.concatenate([conv_w, conv_b[None, :],
                            jnp.zeros((SUBLANES - CONV_WIDTH - 1, 2 * F), conv_w.dtype)], axis=0)
    cw_r = regroup(taps)
    return win_r, cw_r, w_out.astype(jnp.bfloat16)


def kernel(x, meta, pool_w, pool_scale, w_kv, w_f, b_f, w_q, w_o, ffn_w_in, ffn_conv_w, ffn_conv_b,
           ffn_w_out, ln_g, ln_b):
    B, S, D = x.shape
    assert D == D_MODEL and S % TM == 0 and meta.shape == (N_META, D_MODEL)
    assert pool_w.shape[0] == N_A_LAYERS == 1 and w_q.shape[0] == DEPTH - N_A_LAYERS == 1
    tpb = S // TM

    meta_tile = jnp.concatenate([jnp.zeros((META_ROW0, D), x.dtype), meta.astype(x.dtype)], axis=0)
    row2 = lambda a: a.reshape(1, -1)

    h = _pool_ln(x.reshape(B * S, D), meta_tile, pool_w[0].astype(jnp.bfloat16), row2(pool_scale[0]),
                 row2(ln_g[0, 0]), row2(ln_b[0, 0]), tpb)
    h = _ffn_ln(h, *_ffn_weights(ffn_w_in[0], ffn_conv_w[0], ffn_conv_b[0], ffn_w_out[0]),
                row2(ln_g[0, 1]), row2(ln_b[0, 1]), tpb, drop_meta=False)

    wf_pad = jnp.pad(w_f, ((0, 0), (0, LANES - N_HEADS))).astype(jnp.bfloat16)
    bf_pad = jnp.pad(b_f, (0, LANES - N_HEADS)).reshape(1, LANES)
    q, k, v, qa, ka = _qkv(h, w_q[0].astype(jnp.bfloat16), w_kv.astype(jnp.bfloat16), wf_pad, bf_pad, tpb)
    o = _attention(q, qa, k, ka, v, tpb)
    h = _oproj_ln(h, o, w_o[0].astype(jnp.bfloat16), row2(ln_g[1, 0]), row2(ln_b[1, 0]))
    h = _ffn_ln(h, *_ffn_weights(ffn_w_in[1], ffn_conv_w[1], ffn_conv_b[1], ffn_w_out[1]),
                row2(ln_g[1, 1]), row2(ln_b[1, 1]), tpb, drop_meta=True)
    return h.reshape(B, S, D)
```

```python
import functools
import math

import numpy as np
import jax
import jax.numpy as jnp
from jax import lax
from jax.experimental import pallas as pl
from jax.experimental.pallas import tpu as pltpu

D_MODEL = 1024
DEPTH = 2
N_META = 16
POOL_WINDOWS = (2, 4, 8, 16)
N_POOL_GROUPS = len(POOL_WINDOWS)
POOL_GROUP_DIM = D_MODEL // N_POOL_GROUPS
HEAD_DIM = 64
N_HEADS = D_MODEL // HEAD_DIM
D_FF = ((8 * D_MODEL // 3 + 127) // 128) * 128
CONV_WIDTH = 3
N_A_LAYERS = DEPTH // 2
ALPHA = (2.0 * DEPTH) ** 0.25
LN_EPS = 1e-5
NEG_INF = -1e30
LOG2E = math.log2(math.e)

TM = 512
META_ROW0 = TM - N_META
HALO = 16
FFN_CHUNK = 256
N_FFN_CHUNKS = D_FF // FFN_CHUNK
LANES = 128
SUBLANES = 8
AUG = LANES // N_HEADS
V_ROWS = HEAD_DIM + 16
VMEM_LIMIT_BYTES = 56 * 1024 * 1024

assert D_FF % FFN_CHUNK == 0 and AUG >= 6


def _layer_norm(y, g, b):
    mu = jnp.mean(y, axis=-1, keepdims=True)
    xc = y - mu
    var = jnp.mean(xc * xc, axis=-1, keepdims=True)
    return xc * lax.rsqrt(var + LN_EPS) * g + b


def _const_spec(shape):
    nd = len(shape)
    return pl.BlockSpec(shape, lambda *_: (0,) * nd, pipeline_mode=pl.Buffered(1))


def _params(n_axes):
    return pltpu.CompilerParams(dimension_semantics=("arbitrary",) * n_axes,
                                vmem_limit_bytes=VMEM_LIMIT_BYTES)


def _pool_ln_kernel(tiles_per_batch, x_ref, xhalo_ref, meta_ref, w_ref, scale_ref, g_ref, b_ref, o_ref):
    i = pl.program_id(0)
    is_meta = i == 0
    meta_tile = meta_ref[...]
    h = jnp.where(is_meta, meta_tile, x_ref[...])
    batch_start = (i - 1) % tiles_per_batch == 0
    halo = jnp.where(batch_start, meta_tile[TM - HALO:, :], xhalo_ref[...])
    halo = jnp.where(is_meta, 0.0, halo)
    row = lax.broadcasted_iota(jnp.int32, (TM, 1), 0)
    t = jnp.where(is_meta, row - (META_ROW0 - 1), HALO)
    outs = []
    G = POOL_GROUP_DIM
    for gi, w in enumerate(POOL_WINDOWS):
        sl = slice(gi * G, (gi + 1) * G)
        hg = h[:, sl]
        s = jnp.concatenate([halo[:, sl], hg], axis=0)
        k = 1
        while k < w:
            s = s + pltpu.roll(s, k, axis=0)
            k *= 2
        count = jnp.clip(t, 1, w).astype(jnp.float32)
        pooled = s[HALO:, :] / count
        diff = (pooled - hg).astype(jnp.bfloat16)
        mixed = jnp.dot(diff, w_ref[gi], preferred_element_type=jnp.float32)
        outs.append(ALPHA * hg + mixed * scale_ref[:, sl])
    y = jnp.concatenate(outs, axis=1)
    o_ref[...] = _layer_norm(y, g_ref[...], b_ref[...])


def _pool_ln(x_flat, meta_tile, w_bf16, scale, g, b, tiles_per_batch):
    n_tiles = x_flat.shape[0] // TM + 1
    rows = n_tiles * TM
    halo_per_tile = TM // HALO

    def halo_map(i):
        blk = jnp.maximum((i - 1) * halo_per_tile - 1, 0)
        return (jnp.where((i - 1) % tiles_per_batch == 0, 0, blk), 0)

    return pl.pallas_call(
        functools.partial(_pool_ln_kernel, tiles_per_batch),
        grid=(n_tiles,),
        in_specs=[
            pl.BlockSpec((TM, D_MODEL), lambda i: (jnp.maximum(i - 1, 0), 0)),
            pl.BlockSpec((HALO, D_MODEL), halo_map),
            _const_spec(meta_tile.shape),
            _const_spec(w_bf16.shape),
            _const_spec(scale.shape),
            _const_spec(g.shape),
            _const_spec(b.shape),
        ],
        out_specs=pl.BlockSpec((TM, D_MODEL), lambda i: (i, 0)),
        out_shape=jax.ShapeDtypeStruct((rows, D_MODEL), jnp.float32),
        compiler_params=_params(1),
        name="pool_ln",
    )(x_flat, x_flat, meta_tile, w_bf16, scale, g, b)


def _ffn_kernel(tiles_per_batch, h_ref, win_ref, cw_ref, wout_ref, g_ref, b_ref, o_ref,
                act_ref, carry_ref, meta_tail_ref):
    i = pl.program_id(0)

    @pl.when(i == 0)
    def _():
        carry_ref[...] = jnp.zeros_like(carry_ref)
        meta_tail_ref[...] = jnp.zeros_like(meta_tail_ref)

    h = h_ref[...]
    row = lax.broadcasted_iota(jnp.int32, (TM, 1), 0)
    valid = jnp.logical_or(i > 0, row >= META_ROW0)
    hb = jnp.where(valid, h, 0.0).astype(jnp.bfloat16)
    batch_start = jnp.logical_and(i > 0, (i - 1) % tiles_per_batch == 0)
    FC = FFN_CHUNK
    for c in range(N_FFN_CHUNKS):
        cols = slice(c * 2 * FC, (c + 1) * 2 * FC)
        u = jnp.dot(hb, win_ref[c], preferred_element_type=jnp.float32)
        prev = jnp.where(batch_start, meta_tail_ref[:, cols], carry_ref[:, cols])
        p1 = prev[SUBLANES - 1:SUBLANES, :]
        p2 = prev[SUBLANES - 2:SUBLANES - 1, :]
        u1 = jnp.where(row == 0, p1, pltpu.roll(u, 1, axis=0))
        u2 = jnp.where(row == 0, p2, jnp.where(row == 1, p1, pltpu.roll(u, 2, axis=0)))
        cw = cw_ref[c]
        cv = cw[3:4, :] + cw[0:1, :] * u2 + cw[1:2, :] * u1 + cw[2:3, :] * u
        tail = u[TM - SUBLANES:, :]
        carry_ref[:, cols] = tail
        meta_tail_ref[:, cols] = jnp.where(i == 0, tail, meta_tail_ref[:, cols])

        a = cv[:, :FC]
        gate = cv[:, FC:]
        act_ref[:, c * FC:(c + 1) * FC] = (a * jax.nn.sigmoid(a) * gate).astype(jnp.bfloat16)
    ffn = jnp.dot(act_ref[...], wout_ref[...], preferred_element_type=jnp.float32)
    o_ref[...] = _layer_norm(ALPHA * h + ffn, g_ref[...], b_ref[...])


def _ffn_ln(h, win_r, cw_r, wout, g, b, tiles_per_batch, drop_meta):
    rows = h.shape[0]
    n_tiles = rows // TM
    if drop_meta:
        out_rows, out_map = rows - TM, (lambda i: (jnp.maximum(i - 1, 0), 0))
    else:
        out_rows, out_map = rows, (lambda i: (i, 0))
    return pl.pallas_call(
        functools.partial(_ffn_kernel, tiles_per_batch),
        grid=(n_tiles,),
        in_specs=[
            pl.BlockSpec((TM, D_MODEL), lambda i: (i, 0)),
            _const_spec(win_r.shape),
            _const_spec(cw_r.shape),
            _const_spec(wout.shape),
            _const_spec(g.shape),
            _const_spec(b.shape),
        ],
        out_specs=pl.BlockSpec((TM, D_MODEL), out_map),
        out_shape=jax.ShapeDtypeStruct((out_rows, D_MODEL), jnp.float32),
        scratch_shapes=[
            pltpu.VMEM((TM, D_FF), jnp.bfloat16),
            pltpu.VMEM((SUBLANES, 2 * D_FF), jnp.float32),
            pltpu.VMEM((SUBLANES, 2 * D_FF), jnp.float32),
        ],
        compiler_params=_params(1),
        name="ffn_ln",
    )(h, win_r, cw_r, wout, g, b)


def _qkv_kernel(tiles_per_batch, h_ref, wq_ref, wk_ref, wvt_ref, wf_ref, bf_ref,
                q_ref, k_ref, vt_ref, qa_ref, ka_ref, carry_ref, meta_c_ref):
    i = pl.program_id(0)

    @pl.when(i == 0)
    def _():
        carry_ref[...] = jnp.zeros_like(carry_ref)

    hb = h_ref[...].astype(jnp.bfloat16)
    q = jnp.dot(hb, wq_ref[...], preferred_element_type=jnp.float32)
    q_ref[...] = (q * (HEAD_DIM ** -0.5 * LOG2E)).astype(jnp.bfloat16)
    k_ref[...] = jnp.dot(hb, wk_ref[...], preferred_element_type=jnp.float32).astype(jnp.bfloat16)
    vt = lax.dot_general(wvt_ref[...], hb, (((1,), (1,)), ((), ())),
                         preferred_element_type=jnp.float32).astype(jnp.bfloat16)
    pad_row = lax.broadcasted_iota(jnp.int32, (V_ROWS - HEAD_DIM, TM), 0)
    ones_block = (pad_row == 0).astype(jnp.bfloat16)
    for hd in range(N_HEADS):
        vt_ref[hd * V_ROWS:hd * V_ROWS + HEAD_DIM, :] = vt[hd * HEAD_DIM:(hd + 1) * HEAD_DIM, :]
        vt_ref[hd * V_ROWS + HEAD_DIM:(hd + 1) * V_ROWS, :] = ones_block

    z = jnp.dot(hb, wf_ref[...], preferred_element_type=jnp.float32) + bf_ref[...]
    logf = jnp.minimum(z, 0.0) - jnp.log1p(jnp.exp(-jnp.abs(z)))
    row = lax.broadcasted_iota(jnp.int32, (TM, 1), 0)
    valid = jnp.logical_or(i > 0, row >= META_ROW0)
    x = jnp.where(valid, logf, 0.0)
    s = 1
    while s < TM:
        x = x + jnp.where(row >= s, pltpu.roll(x, s, axis=0), 0.0)
        s *= 2
    batch_start = jnp.logical_and(i > 0, (i - 1) % tiles_per_batch == 0)
    last = SUBLANES - 1
    base = jnp.where(batch_start, meta_c_ref[last:, :], carry_ref[last:, :])
    c = x + base
    tail = c[TM - SUBLANES:, :]
    carry_ref[...] = tail

    @pl.when(i == 0)
    def _():
        meta_c_ref[...] = tail

    c2 = c * LOG2E
    hi = c2.astype(jnp.bfloat16)
    r1 = c2 - hi.astype(jnp.float32)
    mid = r1.astype(jnp.bfloat16)
    lo = (r1 - mid.astype(jnp.float32)).astype(jnp.bfloat16)
    src = lax.broadcasted_iota(jnp.int32, (LANES, LANES), 0)
    dst = lax.broadcasted_iota(jnp.int32, (LANES, LANES), 1)

    def spread(piece, offset):
        sel = jnp.logical_and(dst == AUG * src + offset, src < N_HEADS).astype(jnp.bfloat16)
        return jnp.dot(piece, sel, preferred_element_type=jnp.float32)

    slot = lax.broadcasted_iota(jnp.int32, (1, LANES), 1) % AUG
    ones_q = jnp.logical_and(slot >= 3, slot < 6).astype(jnp.float32)
    ones_k = (slot < 3).astype(jnp.float32)
    qa_ref[...] = (spread(hi, 0) + spread(mid, 1) + spread(lo, 2) + ones_q).astype(jnp.bfloat16)
    ka_ref[...] = (ones_k - (spread(hi, 3) + spread(mid, 4) + spread(lo, 5))).astype(jnp.bfloat16)


def _qkv(h, wq, wk, wvt, wf_pad, bf_pad, tiles_per_batch):
    rows = h.shape[0]
    n_tiles = rows // TM
    tile = pl.BlockSpec((TM, D_MODEL), lambda i: (i, 0))
    aug_tile = pl.BlockSpec((TM, LANES), lambda i: (i, 0))
    wide = jax.ShapeDtypeStruct((rows, D_MODEL), jnp.bfloat16)
    narrow = jax.ShapeDtypeStruct((rows, LANES), jnp.bfloat16)
    return pl.pallas_call(
        functools.partial(_qkv_kernel, tiles_per_batch),
        grid=(n_tiles,),
        in_specs=[tile, _const_spec(wq.shape), _const_spec(wk.shape), _const_spec(wvt.shape),
                  _const_spec(wf_pad.shape), _const_spec(bf_pad.shape)],
        out_specs=[tile, tile, pl.BlockSpec((N_HEADS * V_ROWS, TM), lambda i: (0, i)), aug_tile, aug_tile],
        out_shape=[wide, wide, jax.ShapeDtypeStruct((N_HEADS * V_ROWS, rows), jnp.bfloat16), narrow, narrow],
        scratch_shapes=[pltpu.VMEM((SUBLANES, LANES), jnp.float32),
                        pltpu.VMEM((SUBLANES, LANES), jnp.float32)],
        compiler_params=_params(1),
        name="qkv_proj",
    )(h, wq, wk, wvt, wf_pad, bf_pad)


def _attention_steps(n_tiles, tiles_per_batch):
    f_tab, kv_tab, first, last, kmin, diag = [], [], [], [], [], []
    for f in range(n_tiles):
        if f == 0:
            kvs = [0]
        else:
            b, qi = divmod(f - 1, tiles_per_batch)
            kvs = [0] + [1 + b * tiles_per_batch + t for t in range(qi + 1)]
        for n, kv in enumerate(kvs):
            f_tab.append(f)
            kv_tab.append(kv)
            first.append(int(n == 0))
            last.append(int(n == len(kvs) - 1))
            is_diag = kv == f
            diag.append(int(is_diag))
            kmin.append(META_ROW0 if kv == 0 else (0 if is_diag else -1))
    return [np.asarray(t, np.int32) for t in (f_tab, kv_tab, first, last, kmin, diag)]


def _attn_kernel(f_tab, kv_tab, first_tab, last_tab, kmin_tab, diag_tab,
                 q_ref, qa_ref, k_ref, ka_ref, vt_ref, o_ref, m_ref, acc_ref, qx_ref):
    del f_tab, kv_tab
    step_id = pl.program_id(0)
    kmin = kmin_tab[step_id]
    nt_dims = (((1,), (1,)), ((), ()))

    @pl.when(first_tab[step_id] == 1)
    def _():
        m_ref[...] = jnp.full_like(m_ref, NEG_INF)
        acc_ref[...] = jnp.zeros_like(acc_ref)
        lane = lax.broadcasted_iota(jnp.int32, (1, LANES), 1)
        qa = qa_ref[...]
        for hd in range(N_HEADS):
            pair, hh = divmod(hd, 2)
            qp = q_ref[:, pair * LANES:(pair + 1) * LANES]
            in_head = jnp.logical_and(lane >= hh * HEAD_DIM, lane < (hh + 1) * HEAD_DIM)
            aug_lanes = jnp.logical_and(lane >= AUG * hd, lane < AUG * (hd + 1))
            qx_ref[hd, :, :LANES] = jnp.where(in_head, qp, jnp.zeros_like(qp))
            qx_ref[hd, :, LANES:] = jnp.where(aug_lanes, qa, jnp.zeros_like(qa))

    def step(masked, k0):
        tk = TM - k0
        if masked:
            krow = lax.broadcasted_iota(jnp.int32, (tk, TM), 0) + k0
            qcol = lax.broadcasted_iota(jnp.int32, (tk, TM), 1)
            causal = diag_tab[step_id] == 1
            keep = jnp.logical_and(krow >= kmin, jnp.logical_or(krow <= qcol, jnp.logical_not(causal)))
        ka = ka_ref[k0:, :]
        n_pairs = N_HEADS // 2

        def pair_scores(p):
            k_ext = jnp.concatenate([k_ref[k0:, p * LANES:(p + 1) * LANES], ka], axis=1)
            return [lax.dot_general(k_ext, qx_ref[2 * p + hh], nt_dims, preferred_element_type=jnp.float32)
                    for hh in range(2)]

        def softmax_pv(hd, st):
            if masked:
                st = jnp.where(keep, st, NEG_INF)
            m_prev = m_ref[hd:hd + 1, :]
            m_new = jnp.maximum(m_prev, jnp.max(st, axis=0, keepdims=True))
            alpha = jnp.exp2(m_prev - m_new)
            pt = jnp.exp2(st - m_new).astype(jnp.bfloat16)
            pvt = jnp.dot(vt_ref[hd * V_ROWS:(hd + 1) * V_ROWS, k0:], pt,
                          preferred_element_type=jnp.float32)
            acc_ref[hd] = alpha * acc_ref[hd] + pvt
            m_ref[hd:hd + 1, :] = m_new

        cur = pair_scores(0)
        for p in range(n_pairs):
            nxt = pair_scores(p + 1) if p + 1 < n_pairs else None
            for hh in range(2):
                softmax_pv(2 * p + hh, cur[hh])
            cur = nxt

    @pl.when(kmin > 0)
    def _():
        step(True, TM - LANES)

    @pl.when(kmin == 0)
    def _():
        step(True, 0)

    @pl.when(kmin < 0)
    def _():
        step(False, 0)

    @pl.when(last_tab[step_id] == 1)
    def _():
        for p in range(N_HEADS // 2):
            outs = []
            for hd in (2 * p, 2 * p + 1):
                a = acc_ref[hd]
                outs.append(a[:HEAD_DIM, :] / a[HEAD_DIM:HEAD_DIM + 1, :])
            pair_t = jnp.concatenate(outs, axis=0)
            o_ref[:, p * LANES:(p + 1) * LANES] = pair_t.T.astype(o_ref.dtype)


def _attention(q, qa, k, ka, vt, tiles_per_batch):
    rows = q.shape[0]
    tabs = _attention_steps(rows // TM, tiles_per_batch)
    q_map = lambda s, f_tab, kv_tab, *_: (f_tab[s], 0)
    kv_map = lambda s, f_tab, kv_tab, *_: (kv_tab[s], 0)
    grid_spec = pltpu.PrefetchScalarGridSpec(
        num_scalar_prefetch=len(tabs),
        grid=(len(tabs[0]),),
        in_specs=[
            pl.BlockSpec((TM, D_MODEL), q_map),
            pl.BlockSpec((TM, LANES), q_map),
            pl.BlockSpec((TM, D_MODEL), kv_map),
            pl.BlockSpec((TM, LANES), kv_map),
            pl.BlockSpec((N_HEADS * V_ROWS, TM), lambda s, f_tab, kv_tab, *_: (0, kv_tab[s])),
        ],
        out_specs=pl.BlockSpec((TM, D_MODEL), q_map),
        scratch_shapes=[
            pltpu.VMEM((N_HEADS, TM), jnp.float32),
            pltpu.VMEM((N_HEADS, V_ROWS, TM), jnp.float32),
            pltpu.VMEM((N_HEADS, TM, 2 * LANES), jnp.bfloat16),
        ],
    )
    return pl.pallas_call(
        _attn_kernel,
        grid_spec=grid_spec,
        out_shape=jax.ShapeDtypeStruct((rows, D_MODEL), jnp.bfloat16),
        compiler_params=_params(1),
        name="fox_attention",
    )(*[jnp.asarray(t) for t in tabs], q, qa, k, ka, vt)


def _oproj_ln_kernel(h_ref, o_ref, wo_ref, g_ref, b_ref, out_ref):
    mix = jnp.dot(o_ref[...], wo_ref[...], preferred_element_type=jnp.float32)
    out_ref[...] = _layer_norm(ALPHA * h_ref[...] + mix, g_ref[...], b_ref[...])


def _oproj_ln(h, o, wo, g, b):
    rows = h.shape[0]
    tile = pl.BlockSpec((TM, D_MODEL), lambda i: (i, 0))
    return pl.pallas_call(
        _oproj_ln_kernel,
        grid=(rows // TM,),
        in_specs=[tile, tile, _const_spec(wo.shape), _const_spec(g.shape), _const_spec(b.shape)],
        out_specs=tile,
        out_shape=jax.ShapeDtypeStruct((rows, D_MODEL), jnp.float32),
        compiler_params=_params(1),
        name="oproj_ln",
    )(h, o, wo, g, b)


def _ffn_weights(w_in, conv_w, conv_b, w_out):
    F, FC, NC = D_FF, FFN_CHUNK, N_FFN_CHUNKS

    def regroup(m):
        a = m[..., :F].reshape(m.shape[:-1] + (NC, FC))
        g = m[..., F:].reshape(m.shape[:-1] + (NC, FC))
        return jnp.moveaxis(jnp.concatenate([a, g], axis=-1), -2, 0)

    win_r = regroup(w_in).astype(jnp.bfloat16)
    taps = jnp.concatenate([conv_w, conv_b[None, :],
                            jnp.zeros((SUBLANES - CONV_WIDTH - 1, 2 * F), conv_w.dtype)], axis=0)
    cw_r = regroup(taps)
    return win_r, cw_r, w_out.astype(jnp.bfloat16)


def kernel(x, meta, pool_w, pool_scale, w_kv, w_f, b_f, w_q, w_o, ffn_w_in, ffn_conv_w, ffn_conv_b,
           ffn_w_out, ln_g, ln_b):
    B, S, D = x.shape
    assert D == D_MODEL and S % TM == 0 and meta.shape == (N_META, D_MODEL)
    assert pool_w.shape[0] == N_A_LAYERS == 1 and w_q.shape[0] == DEPTH - N_A_LAYERS == 1
    tpb = S // TM

    meta_tile = jnp.concatenate([jnp.zeros((META_ROW0, D), x.dtype), meta.astype(x.dtype)], axis=0)
    row2 = lambda a: a.reshape(1, -1)

    h = _pool_ln(x.reshape(B * S, D), meta_tile, pool_w[0].astype(jnp.bfloat16), row2(pool_scale[0]),
                 row2(ln_g[0, 0]), row2(ln_b[0, 0]), tpb)
    h = _ffn_ln(h, *_ffn_weights(ffn_w_in[0], ffn_conv_w[0], ffn_conv_b[0], ffn_w_out[0]),
                row2(ln_g[0, 1]), row2(ln_b[0, 1]), tpb, drop_meta=False)

    wf_pad = jnp.pad(w_f, ((0, 0), (0, LANES - N_HEADS))).astype(jnp.bfloat16)
    bf_pad = jnp.pad(b_f, (0, LANES - N_HEADS)).reshape(1, LANES)
    wk = w_kv[:, :D].astype(jnp.bfloat16)
    wvt = w_kv[:, D:].T.astype(jnp.bfloat16)
    q, k, vt, qa, ka = _qkv(h, w_q[0].astype(jnp.bfloat16), wk, wvt, wf_pad, bf_pad, tpb)
    o = _attention(q, qa, k, ka, vt, tpb)
    h = _oproj_ln(h, o, w_o[0].astype(jnp.bfloat16), row2(ln_g[1, 0]), row2(ln_b[1, 0]))
    h = _ffn_ln(h, *_ffn_weights(ffn_w_in[1], ffn_conv_w[1], ffn_conv_b[1], ffn_w_out[1]),
                row2(ln_g[1, 1]), row2(ln_b[1, 1]), tpb, drop_meta=True)
    return h.reshape(B, S, D)
```

```python
import functools
import math

import numpy as np
import jax
import jax.numpy as jnp
from jax import lax
from jax.experimental import pallas as pl
from jax.experimental.pallas import tpu as pltpu

D_MODEL = 1024
DEPTH = 2
N_META = 16
POOL_WINDOWS = (2, 4, 8, 16)
N_POOL_GROUPS = len(POOL_WINDOWS)
POOL_GROUP_DIM = D_MODEL // N_POOL_GROUPS
HEAD_DIM = 64
N_HEADS = D_MODEL // HEAD_DIM
D_FF = ((8 * D_MODEL // 3 + 127) // 128) * 128
CONV_WIDTH = 3
N_A_LAYERS = DEPTH // 2
ALPHA = (2.0 * DEPTH) ** 0.25
LN_EPS = 1e-5
NEG_INF = -1e30
LOG2E = math.log2(math.e)

TM = 512
META_ROW0 = TM - N_META
HALO = 16
FFN_CHUNK = 256
N_FFN_CHUNKS = D_FF // FFN_CHUNK
LANES = 128
SUBLANES = 8
AUG = LANES // N_HEADS
VMEM_LIMIT_BYTES = 56 * 1024 * 1024

assert D_FF % FFN_CHUNK == 0 and AUG >= 6


def _layer_norm(y, g, b):
    mu = jnp.mean(y, axis=-1, keepdims=True)
    xc = y - mu
    var = jnp.mean(xc * xc, axis=-1, keepdims=True)
    return xc * lax.rsqrt(var + LN_EPS) * g + b


def _const_spec(shape):
    nd = len(shape)
    return pl.BlockSpec(shape, lambda *_: (0,) * nd, pipeline_mode=pl.Buffered(1))


def _params(n_axes):
    return pltpu.CompilerParams(dimension_semantics=("arbitrary",) * n_axes,
                                vmem_limit_bytes=VMEM_LIMIT_BYTES)


def _pool_ln_kernel(tiles_per_batch, x_ref, xhalo_ref, meta_ref, w_ref, scale_ref, g_ref, b_ref, o_ref):
    i = pl.program_id(0)
    is_meta = i == 0
    meta_tile = meta_ref[...]
    h = jnp.where(is_meta, meta_tile, x_ref[...])
    batch_start = (i - 1) % tiles_per_batch == 0
    halo = jnp.where(batch_start, meta_tile[TM - HALO:, :], xhalo_ref[...])
    halo = jnp.where(is_meta, 0.0, halo)
    row = lax.broadcasted_iota(jnp.int32, (TM, 1), 0)
    t = jnp.where(is_meta, row - (META_ROW0 - 1), HALO)
    outs = []
    G = POOL_GROUP_DIM
    for gi, w in enumerate(POOL_WINDOWS):
        sl = slice(gi * G, (gi + 1) * G)
        hg = h[:, sl]
        s = jnp.concatenate([halo[:, sl], hg], axis=0)
        k = 1
        while k < w:
            s = s + pltpu.roll(s, k, axis=0)
            k *= 2
        count = jnp.clip(t, 1, w).astype(jnp.float32)
        pooled = s[HALO:, :] / count
        diff = (pooled - hg).astype(jnp.bfloat16)
        mixed = jnp.dot(diff, w_ref[gi], preferred_element_type=jnp.float32)
        outs.append(ALPHA * hg + mixed * scale_ref[:, sl])
    y = jnp.concatenate(outs, axis=1)
    o_ref[...] = _layer_norm(y, g_ref[...], b_ref[...])


def _pool_ln(x_flat, meta_tile, w_bf16, scale, g, b, tiles_per_batch):
    n_tiles = x_flat.shape[0] // TM + 1
    rows = n_tiles * TM
    halo_per_tile = TM // HALO

    def halo_map(i):
        blk = jnp.maximum((i - 1) * halo_per_tile - 1, 0)
        return (jnp.where((i - 1) % tiles_per_batch == 0, 0, blk), 0)

    return pl.pallas_call(
        functools.partial(_pool_ln_kernel, tiles_per_batch),
        grid=(n_tiles,),
        in_specs=[
            pl.BlockSpec((TM, D_MODEL), lambda i: (jnp.maximum(i - 1, 0), 0)),
            pl.BlockSpec((HALO, D_MODEL), halo_map),
            _const_spec(meta_tile.shape),
            _const_spec(w_bf16.shape),
            _const_spec(scale.shape),
            _const_spec(g.shape),
            _const_spec(b.shape),
        ],
        out_specs=pl.BlockSpec((TM, D_MODEL), lambda i: (i, 0)),
        out_shape=jax.ShapeDtypeStruct((rows, D_MODEL), jnp.float32),
        compiler_params=_params(1),
        name="pool_ln",
    )(x_flat, x_flat, meta_tile, w_bf16, scale, g, b)


def _ffn_kernel(tiles_per_batch, h_ref, win_ref, cw_ref, wout_ref, g_ref, b_ref, o_ref,
                act_ref, carry_ref, meta_tail_ref):
    i = pl.program_id(0)

    @pl.when(i == 0)
    def _():
        carry_ref[...] = jnp.zeros_like(carry_ref)
        meta_tail_ref[...] = jnp.zeros_like(meta_tail_ref)

    h = h_ref[...]
    row = lax.broadcasted_iota(jnp.int32, (TM, 1), 0)
    valid = jnp.logical_or(i > 0, row >= META_ROW0)
    hb = jnp.where(valid, h, 0.0).astype(jnp.bfloat16)
    batch_start = jnp.logical_and(i > 0, (i - 1) % tiles_per_batch == 0)
    FC = FFN_CHUNK
    for c in range(N_FFN_CHUNKS):
        cols = slice(c * 2 * FC, (c + 1) * 2 * FC)
        u = jnp.concatenate(
            [jnp.dot(hb, win_ref[:, c * FC:(c + 1) * FC], preferred_element_type=jnp.float32),
             jnp.dot(hb, win_ref[:, D_FF + c * FC:D_FF + (c + 1) * FC], preferred_element_type=jnp.float32)],
            axis=1)
        prev = jnp.where(batch_start, meta_tail_ref[:, cols], carry_ref[:, cols])
        p1 = prev[SUBLANES - 1:SUBLANES, :]
        p2 = prev[SUBLANES - 2:SUBLANES - 1, :]
        u1 = jnp.where(row == 0, p1, pltpu.roll(u, 1, axis=0))
        u2 = jnp.where(row == 0, p2, jnp.where(row == 1, p1, pltpu.roll(u, 2, axis=0)))
        cw = cw_ref[c]
        cv = cw[3:4, :] + cw[0:1, :] * u2 + cw[1:2, :] * u1 + cw[2:3, :] * u
        tail = u[TM - SUBLANES:, :]
        carry_ref[:, cols] = tail
        meta_tail_ref[:, cols] = jnp.where(i == 0, tail, meta_tail_ref[:, cols])

        a = cv[:, :FC]
        gate = cv[:, FC:]
        act_ref[:, c * FC:(c + 1) * FC] = (a * jax.nn.sigmoid(a) * gate).astype(jnp.bfloat16)
    ffn = jnp.dot(act_ref[...], wout_ref[...], preferred_element_type=jnp.float32)
    o_ref[...] = _layer_norm(ALPHA * h + ffn, g_ref[...], b_ref[...])


def _ffn_ln(h, win, cw_r, wout, g, b, tiles_per_batch, drop_meta):
    rows = h.shape[0]
    n_tiles = rows // TM
    if drop_meta:
        out_rows, out_map = rows - TM, (lambda i: (jnp.maximum(i - 1, 0), 0))
    else:
        out_rows, out_map = rows, (lambda i: (i, 0))
    return pl.pallas_call(
        functools.partial(_ffn_kernel, tiles_per_batch),
        grid=(n_tiles,),
        in_specs=[
            pl.BlockSpec((TM, D_MODEL), lambda i: (i, 0)),
            _const_spec(win.shape),
            _const_spec(cw_r.shape),
            _const_spec(wout.shape),
            _const_spec(g.shape),
            _const_spec(b.shape),
        ],
        out_specs=pl.BlockSpec((TM, D_MODEL), out_map),
        out_shape=jax.ShapeDtypeStruct((out_rows, D_MODEL), jnp.float32),
        scratch_shapes=[
            pltpu.VMEM((TM, D_FF), jnp.bfloat16),
            pltpu.VMEM((SUBLANES, 2 * D_FF), jnp.float32),
            pltpu.VMEM((SUBLANES, 2 * D_FF), jnp.float32),
        ],
        compiler_params=_params(1),
        name="ffn_ln",
    )(h, win, cw_r, wout, g, b)


def _qkv_kernel(tiles_per_batch, h_ref, wq_ref, wkv_ref, wf_ref, bf_ref,
                q_ref, k_ref, v_ref, qa_ref, ka_ref, carry_ref, meta_c_ref):
    i = pl.program_id(0)

    @pl.when(i == 0)
    def _():
        carry_ref[...] = jnp.zeros_like(carry_ref)

    hb = h_ref[...].astype(jnp.bfloat16)
    q = jnp.dot(hb, wq_ref[...], preferred_element_type=jnp.float32)
    q_ref[...] = (q * (HEAD_DIM ** -0.5 * LOG2E)).astype(jnp.bfloat16)
    k_ref[...] = jnp.dot(hb, wkv_ref[:, :D_MODEL], preferred_element_type=jnp.float32).astype(jnp.bfloat16)
    v_ref[...] = jnp.dot(hb, wkv_ref[:, D_MODEL:], preferred_element_type=jnp.float32).astype(jnp.bfloat16)

    z = jnp.dot(hb, wf_ref[...], preferred_element_type=jnp.float32) + bf_ref[...]
    logf = jnp.minimum(z, 0.0) - jnp.log1p(jnp.exp(-jnp.abs(z)))
    row = lax.broadcasted_iota(jnp.int32, (TM, 1), 0)
    valid = jnp.logical_or(i > 0, row >= META_ROW0)
    x = jnp.where(valid, logf, 0.0)
    s = 1
    while s < TM:
        x = x + jnp.where(row >= s, pltpu.roll(x, s, axis=0), 0.0)
        s *= 2
    batch_start = jnp.logical_and(i > 0, (i - 1) % tiles_per_batch == 0)
    last = SUBLANES - 1
    base = jnp.where(batch_start, meta_c_ref[last:, :], carry_ref[last:, :])
    c = x + base
    tail = c[TM - SUBLANES:, :]
    carry_ref[...] = tail

    @pl.when(i == 0)
    def _():
        meta_c_ref[...] = tail

    c2 = c * LOG2E
    hi = c2.astype(jnp.bfloat16)
    r1 = c2 - hi.astype(jnp.float32)
    mid = r1.astype(jnp.bfloat16)
    lo = (r1 - mid.astype(jnp.float32)).astype(jnp.bfloat16)
    src = lax.broadcasted_iota(jnp.int32, (LANES, LANES), 0)
    dst = lax.broadcasted_iota(jnp.int32, (LANES, LANES), 1)

    def spread(piece, offset):
        sel = jnp.logical_and(dst == AUG * src + offset, src < N_HEADS).astype(jnp.bfloat16)
        return jnp.dot(piece, sel, preferred_element_type=jnp.float32)

    slot = lax.broadcasted_iota(jnp.int32, (1, LANES), 1) % AUG
    ones_q = jnp.logical_and(slot >= 3, slot < 6).astype(jnp.float32)
    ones_k = (slot < 3).astype(jnp.float32)
    qa_ref[...] = (spread(hi, 0) + spread(mid, 1) + spread(lo, 2) + ones_q).astype(jnp.bfloat16)
    ka_ref[...] = (ones_k - (spread(hi, 3) + spread(mid, 4) + spread(lo, 5))).astype(jnp.bfloat16)


def _qkv(h, wq, wkv, wf_pad, bf_pad, tiles_per_batch):
    rows = h.shape[0]
    n_tiles = rows // TM
    tile = pl.BlockSpec((TM, D_MODEL), lambda i: (i, 0))
    aug_tile = pl.BlockSpec((TM, LANES), lambda i: (i, 0))
    wide = jax.ShapeDtypeStruct((rows, D_MODEL), jnp.bfloat16)
    narrow = jax.ShapeDtypeStruct((rows, LANES), jnp.bfloat16)
    return pl.pallas_call(
        functools.partial(_qkv_kernel, tiles_per_batch),
        grid=(n_tiles,),
        in_specs=[tile, _const_spec(wq.shape), _const_spec(wkv.shape),
                  _const_spec(wf_pad.shape), _const_spec(bf_pad.shape)],
        out_specs=[tile, tile, tile, aug_tile, aug_tile],
        out_shape=[wide, wide, wide, narrow, narrow],
        scratch_shapes=[pltpu.VMEM((SUBLANES, LANES), jnp.float32),
                        pltpu.VMEM((SUBLANES, LANES), jnp.float32)],
        compiler_params=_params(1),
        name="qkv_proj",
    )(h, wq, wkv, wf_pad, bf_pad)


def _attention_steps(n_tiles, tiles_per_batch):
    f_tab, kv_tab, first, last, kmin, diag = [], [], [], [], [], []
    for f in range(n_tiles):
        if f == 0:
            kvs = [0]
        else:
            b, qi = divmod(f - 1, tiles_per_batch)
            kvs = [0] + [1 + b * tiles_per_batch + t for t in range(qi + 1)]
        for n, kv in enumerate(kvs):
            f_tab.append(f)
            kv_tab.append(kv)
            first.append(int(n == 0))
            last.append(int(n == len(kvs) - 1))
            is_diag = kv == f
            diag.append(int(is_diag))
            kmin.append(META_ROW0 if kv == 0 else (0 if is_diag else -1))
    return [np.asarray(t, np.int32) for t in (f_tab, kv_tab, first, last, kmin, diag)]


def _attn_kernel(f_tab, kv_tab, first_tab, last_tab, kmin_tab, diag_tab,
                 q_ref, qa_ref, k_ref, ka_ref, v_ref, o_ref, m_ref, acc_ref):
    del f_tab, kv_tab
    step_id = pl.program_id(0)
    kmin = kmin_tab[step_id]

    @pl.when(first_tab[step_id] == 1)
    def _():
        m_ref[...] = jnp.full_like(m_ref, NEG_INF)
        acc_ref[...] = jnp.zeros_like(acc_ref)

    lane = lax.broadcasted_iota(jnp.int32, (1, LANES), 1)
    low_half = lane < HEAD_DIM
    nt_dims = (((1,), (1,)), ((), ()))

    def step(masked, k0):
        tk = TM - k0
        n_kblk = tk // LANES
        if masked:
            qrow = lax.broadcasted_iota(jnp.int32, (TM, tk), 0)
            krow = lax.broadcasted_iota(jnp.int32, (TM, tk), 1) + k0
            causal = diag_tab[step_id] == 1
            keep = jnp.logical_and(krow >= kmin, jnp.logical_or(krow <= qrow, jnp.logical_not(causal)))
        qa = qa_ref[...]
        ka = ka_ref[k0:, :]
        n_pairs = N_HEADS // 2

        def pair_scores(p):
            cols = slice(p * LANES, (p + 1) * LANES)
            qp = q_ref[:, cols]
            k_ext = jnp.concatenate([k_ref[k0:, cols], ka], axis=1)
            vp = v_ref[k0:, cols]
            scores, vms = [], []
            for hh in range(2):
                hd = 2 * p + hh
                in_head = low_half if hh == 0 else jnp.logical_not(low_half)
                ones_lane = HEAD_DIM if hh == 0 else 0
                aug_lanes = jnp.logical_and(lane >= AUG * hd, lane < AUG * (hd + 1))
                q_ext = jnp.concatenate([jnp.where(in_head, qp, jnp.zeros_like(qp)),
                                         jnp.where(aug_lanes, qa, jnp.zeros_like(qa))], axis=1)
                vms.append(jnp.where(in_head, vp, jnp.where(lane == ones_lane, 1.0, 0.0).astype(vp.dtype)))
                scores.append(lax.dot_general(q_ext, k_ext, nt_dims, preferred_element_type=jnp.float32))
            return scores, vms

        def softmax_pv(hd, s, vm):
            m_prev = m_ref[hd]
            if masked and k0 == 0:
                tri = (lax.broadcasted_iota(jnp.int32, (LANES, LANES), 1)
                       <= lax.broadcasted_iota(jnp.int32, (LANES, LANES), 0))
                blk = lambda qb, kb: s[qb * LANES:(qb + 1) * LANES, kb * LANES:(kb + 1) * LANES]
                diag_blk = [jnp.where(tri, blk(b, b), NEG_INF) for b in range(n_kblk)]
                m_rows = []
                for qb in range(n_kblk):
                    m_part = diag_blk[qb]
                    for kb in range(qb):
                        m_part = jnp.maximum(m_part, blk(qb, kb))
                    m_rows.append(jnp.max(m_part, axis=1, keepdims=True))
                m_new = jnp.maximum(m_prev, jnp.concatenate(m_rows, axis=0))
                p_rows = []
                for qb in range(n_kblk):
                    mq = m_new[qb * LANES:(qb + 1) * LANES, :]
                    p_rows.append(jnp.concatenate(
                        [jnp.exp2(blk(qb, kb) - mq) for kb in range(qb)]
                        + [jnp.exp2(diag_blk[qb] - mq)]
                        + [jnp.zeros((LANES, LANES), jnp.float32)] * (n_kblk - 1 - qb), axis=1))
                pe = jnp.concatenate(p_rows, axis=0).astype(jnp.bfloat16)
            else:
                if masked:
                    s = jnp.where(keep, s, NEG_INF)
                m_part = s[:, :LANES]
                for kb in range(1, n_kblk):
                    m_part = jnp.maximum(m_part, s[:, kb * LANES:(kb + 1) * LANES])
                m_new = jnp.maximum(m_prev, jnp.max(m_part, axis=1, keepdims=True))
                pe = jnp.concatenate([jnp.exp2(s[:, kb * LANES:(kb + 1) * LANES] - m_new)
                                      for kb in range(n_kblk)], axis=1).astype(jnp.bfloat16)
            alpha = jnp.exp2(m_prev - m_new)
            pv = jnp.dot(pe, vm, preferred_element_type=jnp.float32)
            acc_ref[hd] = alpha * acc_ref[hd] + pv
            m_ref[hd] = m_new

        cur = pair_scores(0)
        for p in range(n_pairs):
            nxt = pair_scores(p + 1) if p + 1 < n_pairs else None
            for hh in range(2):
                softmax_pv(2 * p + hh, cur[0][hh], cur[1][hh])
            cur = nxt

    @pl.when(kmin > 0)
    def _():
        step(True, TM - LANES)

    @pl.when(kmin == 0)
    def _():
        step(True, 0)

    @pl.when(kmin < 0)
    def _():
        step(False, 0)

    @pl.when(last_tab[step_id] == 1)
    def _():
        for p in range(N_HEADS // 2):
            cols = slice(p * LANES, (p + 1) * LANES)
            a0 = acc_ref[2 * p]
            a1 = acc_ref[2 * p + 1]
            l0 = a0[:, HEAD_DIM:HEAD_DIM + 1]
            l1 = a1[:, 0:1]
            o_ref[:, cols] = jnp.where(low_half, a0 / l0, a1 / l1).astype(o_ref.dtype)


def _attention(q, qa, k, ka, v, tiles_per_batch):
    rows = q.shape[0]
    tabs = _attention_steps(rows // TM, tiles_per_batch)
    q_map = lambda s, f_tab, kv_tab, *_: (f_tab[s], 0)
    kv_map = lambda s, f_tab, kv_tab, *_: (kv_tab[s], 0)
    grid_spec = pltpu.PrefetchScalarGridSpec(
        num_scalar_prefetch=len(tabs),
        grid=(len(tabs[0]),),
        in_specs=[
            pl.BlockSpec((TM, D_MODEL), q_map),
            pl.BlockSpec((TM, LANES), q_map),
            pl.BlockSpec((TM, D_MODEL), kv_map),
            pl.BlockSpec((TM, LANES), kv_map),
            pl.BlockSpec((TM, D_MODEL), kv_map),
        ],
        out_specs=pl.BlockSpec((TM, D_MODEL), q_map),
        scratch_shapes=[
            pltpu.VMEM((N_HEADS, TM, LANES), jnp.float32),
            pltpu.VMEM((N_HEADS, TM, LANES), jnp.float32),
        ],
    )
    return pl.pallas_call(
        _attn_kernel,
        grid_spec=grid_spec,
        out_shape=jax.ShapeDtypeStruct((rows, D_MODEL), jnp.bfloat16),
        compiler_params=_params(1),
        name="fox_attention",
    )(*[jnp.asarray(t) for t in tabs], q, qa, k, ka, v)


def _oproj_ln_kernel(h_ref, o_ref, wo_ref, g_ref, b_ref, out_ref):
    mix = jnp.dot(o_ref[...], wo_ref[...], preferred_element_type=jnp.float32)
    out_ref[...] = _layer_norm(ALPHA * h_ref[...] + mix, g_ref[...], b_ref[...])


def _oproj_ln(h, o, wo, g, b):
    rows = h.shape[0]
    tile = pl.BlockSpec((TM, D_MODEL), lambda i: (i, 0))
    return pl.pallas_call(
        _oproj_ln_kernel,
        grid=(rows // TM,),
        in_specs=[tile, tile, _const_spec(wo.shape), _const_spec(g.shape), _const_spec(b.shape)],
        out_specs=tile,
        out_shape=jax.ShapeDtypeStruct((rows, D_MODEL), jnp.float32),
        compiler_params=_params(1),
        name="oproj_ln",
    )(h, o, wo, g, b)


def _ffn_weights(w_in, conv_w, conv_b, w_out):
    F, FC, NC = D_FF, FFN_CHUNK, N_FFN_CHUNKS

    def regroup(m):
        a = m[..., :F].reshape(m.shape[:-1] + (NC, FC))
        g = m[..., F:].reshape(m.shape[:-1] + (NC, FC))
        return jnp.moveaxis(jnp.concatenate([a, g], axis=-1), -2, 0)

    win = w_in.astype(jnp.bfloat16)
    taps = jnp.concatenate([conv_w, conv_b[None, :],
                            jnp.zeros((SUBLANES - CONV_WIDTH - 1, 2 * F), conv_w.dtype)], axis=0)
    cw_r = regroup(taps)
    return win, cw_r, w_out.astype(jnp.bfloat16)


def kernel(x, meta, pool_w, pool_scale, w_kv, w_f, b_f, w_q, w_o, ffn_w_in, ffn_conv_w, ffn_conv_b,
           ffn_w_out, ln_g, ln_b):
    B, S, D = x.shape
    assert D == D_MODEL and S % TM == 0 and meta.shape == (N_META, D_MODEL)
    assert pool_w.shape[0] == N_A_LAYERS == 1 and w_q.shape[0] == DEPTH - N_A_LAYERS == 1
    tpb = S // TM

    meta_tile = jnp.concatenate([jnp.zeros((META_ROW0, D), x.dtype), meta.astype(x.dtype)], axis=0)
    row2 = lambda a: a.reshape(1, -1)

    h = _pool_ln(x.reshape(B * S, D), meta_tile, pool_w[0].astype(jnp.bfloat16), row2(pool_scale[0]),
                 row2(ln_g[0, 0]), row2(ln_b[0, 0]), tpb)
    h = _ffn_ln(h, *_ffn_weights(ffn_w_in[0], ffn_conv_w[0], ffn_conv_b[0], ffn_w_out[0]),
                row2(ln_g[0, 1]), row2(ln_b[0, 1]), tpb, drop_meta=False)

    wf_pad = jnp.pad(w_f, ((0, 0), (0, LANES - N_HEADS))).astype(jnp.bfloat16)
    bf_pad = jnp.pad(b_f, (0, LANES - N_HEADS)).reshape(1, LANES)
    q, k, v, qa, ka = _qkv(h, w_q[0].astype(jnp.bfloat16), w_kv.astype(jnp.bfloat16), wf_pad, bf_pad, tpb)
    o = _attention(q, qa, k, ka, v, tpb)
    h = _oproj_ln(h, o, w_o[0].astype(jnp.bfloat16), row2(ln_g[1, 0]), row2(ln_b[1, 0]))
    h = _ffn_ln(h, *_ffn_weights(ffn_w_in[1], ffn_conv_w[1], ffn_conv_b[1], ffn_w_out[1]),
                row2(ln_g[1, 1]), row2(ln_b[1, 1]), tpb, drop_meta=True)
    return h.reshape(B, S, D)
```

```python
import functools
import math

import numpy as np
import jax
import jax.numpy as jnp
from jax import lax
from jax.experimental import pallas as pl
from jax.experimental.pallas import tpu as pltpu

D_MODEL = 1024
DEPTH = 2
N_META = 16
POOL_WINDOWS = (2, 4, 8, 16)
N_POOL_GROUPS = len(POOL_WINDOWS)
POOL_GROUP_DIM = D_MODEL // N_POOL_GROUPS
HEAD_DIM = 64
N_HEADS = D_MODEL // HEAD_DIM
D_FF = ((8 * D_MODEL // 3 + 127) // 128) * 128
CONV_WIDTH = 3
N_A_LAYERS = DEPTH // 2
ALPHA = (2.0 * DEPTH) ** 0.25
LN_EPS = 1e-5
NEG_INF = -1e30
LOG2E = math.log2(math.e)

TM = 512
META_ROW0 = TM - N_META
HALO = 16
FFN_CHUNK = 256
N_FFN_CHUNKS = D_FF // FFN_CHUNK
LANES = 128
SUBLANES = 8
AUG = LANES // N_HEADS
VMEM_LIMIT_BYTES = 56 * 1024 * 1024

assert D_FF % FFN_CHUNK == 0 and AUG >= 6


def _layer_norm(y, g, b):
    mu = jnp.mean(y, axis=-1, keepdims=True)
    xc = y - mu
    var = jnp.mean(xc * xc, axis=-1, keepdims=True)
    return xc * lax.rsqrt(var + LN_EPS) * g + b


def _const_spec(shape):
    nd = len(shape)
    return pl.BlockSpec(shape, lambda *_: (0,) * nd, pipeline_mode=pl.Buffered(1))


def _params(n_axes):
    return pltpu.CompilerParams(dimension_semantics=("arbitrary",) * n_axes,
                                vmem_limit_bytes=VMEM_LIMIT_BYTES)


def _pool_ln_kernel(tiles_per_batch, x_ref, xhalo_ref, meta_ref, w_ref, scale_ref, g_ref, b_ref, o_ref):
    i = pl.program_id(0)
    is_meta = i == 0
    meta_tile = meta_ref[...]
    h = jnp.where(is_meta, meta_tile, x_ref[...])
    batch_start = (i - 1) % tiles_per_batch == 0
    halo = jnp.where(batch_start, meta_tile[TM - HALO:, :], xhalo_ref[...])
    halo = jnp.where(is_meta, 0.0, halo)
    row = lax.broadcasted_iota(jnp.int32, (TM, 1), 0)
    t = jnp.where(is_meta, row - (META_ROW0 - 1), HALO)
    outs = []
    G = POOL_GROUP_DIM
    for gi, w in enumerate(POOL_WINDOWS):
        sl = slice(gi * G, (gi + 1) * G)
        hg = h[:, sl]
        s = jnp.concatenate([halo[:, sl], hg], axis=0)
        k = 1
        while k < w:
            s = s + pltpu.roll(s, k, axis=0)
            k *= 2
        count = jnp.clip(t, 1, w).astype(jnp.float32)
        pooled = s[HALO:, :] / count
        diff = (pooled - hg).astype(jnp.bfloat16)
        mixed = jnp.dot(diff, w_ref[gi], preferred_element_type=jnp.float32)
        outs.append(ALPHA * hg + mixed * scale_ref[:, sl])
    y = jnp.concatenate(outs, axis=1)
    o_ref[...] = _layer_norm(y, g_ref[...], b_ref[...])


def _pool_ln(x_flat, meta_tile, w_bf16, scale, g, b, tiles_per_batch):
    n_tiles = x_flat.shape[0] // TM + 1
    rows = n_tiles * TM
    halo_per_tile = TM // HALO

    def halo_map(i):
        blk = jnp.maximum((i - 1) * halo_per_tile - 1, 0)
        return (jnp.where((i - 1) % tiles_per_batch == 0, 0, blk), 0)

    return pl.pallas_call(
        functools.partial(_pool_ln_kernel, tiles_per_batch),
        grid=(n_tiles,),
        in_specs=[
            pl.BlockSpec((TM, D_MODEL), lambda i: (jnp.maximum(i - 1, 0), 0)),
            pl.BlockSpec((HALO, D_MODEL), halo_map),
            _const_spec(meta_tile.shape),
            _const_spec(w_bf16.shape),
            _const_spec(scale.shape),
            _const_spec(g.shape),
            _const_spec(b.shape),
        ],
        out_specs=pl.BlockSpec((TM, D_MODEL), lambda i: (i, 0)),
        out_shape=jax.ShapeDtypeStruct((rows, D_MODEL), jnp.float32),
        compiler_params=_params(1),
        name="pool_ln",
    )(x_flat, x_flat, meta_tile, w_bf16, scale, g, b)


def _ffn_kernel(tiles_per_batch, h_ref, win_ref, cw_ref, wout_ref, g_ref, b_ref, o_ref,
                act_ref, carry_ref, meta_tail_ref):
    i = pl.program_id(0)

    @pl.when(i == 0)
    def _():
        carry_ref[...] = jnp.zeros_like(carry_ref)
        meta_tail_ref[...] = jnp.zeros_like(meta_tail_ref)

    h = h_ref[...]
    row = lax.broadcasted_iota(jnp.int32, (TM, 1), 0)
    valid = jnp.logical_or(i > 0, row >= META_ROW0)
    hb = jnp.where(valid, h, 0.0).astype(jnp.bfloat16)
    batch_start = jnp.logical_and(i > 0, (i - 1) % tiles_per_batch == 0)
    FC = FFN_CHUNK
    for c in range(N_FFN_CHUNKS):
        cols = slice(c * 2 * FC, (c + 1) * 2 * FC)
        u = jnp.concatenate(
            [jnp.dot(hb, win_ref[:, c * FC:(c + 1) * FC], preferred_element_type=jnp.float32),
             jnp.dot(hb, win_ref[:, D_FF + c * FC:D_FF + (c + 1) * FC], preferred_element_type=jnp.float32)],
            axis=1)
        prev = jnp.where(batch_start, meta_tail_ref[:, cols], carry_ref[:, cols])
        p1 = prev[SUBLANES - 1:SUBLANES, :]
        p2 = prev[SUBLANES - 2:SUBLANES - 1, :]
        u1 = jnp.where(row == 0, p1, pltpu.roll(u, 1, axis=0))
        u2 = jnp.where(row == 0, p2, jnp.where(row == 1, p1, pltpu.roll(u, 2, axis=0)))
        cw = cw_ref[c]
        cv = cw[3:4, :] + cw[0:1, :] * u2 + cw[1:2, :] * u1 + cw[2:3, :] * u
        tail = u[TM - SUBLANES:, :]
        carry_ref[:, cols] = tail
        meta_tail_ref[:, cols] = jnp.where(i == 0, tail, meta_tail_ref[:, cols])

        a = cv[:, :FC]
        gate = cv[:, FC:]
        act_ref[:, c * FC:(c + 1) * FC] = (a * jax.nn.sigmoid(a) * gate).astype(jnp.bfloat16)
    ffn = jnp.dot(act_ref[...], wout_ref[...], preferred_element_type=jnp.float32)
    o_ref[...] = _layer_norm(ALPHA * h + ffn, g_ref[...], b_ref[...])


def _ffn_ln(h, win, cw_r, wout, g, b, tiles_per_batch, drop_meta):
    rows = h.shape[0]
    n_tiles = rows // TM
    if drop_meta:
        out_rows, out_map = rows - TM, (lambda i: (jnp.maximum(i - 1, 0), 0))
    else:
        out_rows, out_map = rows, (lambda i: (i, 0))
    return pl.pallas_call(
        functools.partial(_ffn_kernel, tiles_per_batch),
        grid=(n_tiles,),
        in_specs=[
            pl.BlockSpec((TM, D_MODEL), lambda i: (i, 0)),
            _const_spec(win.shape),
            _const_spec(cw_r.shape),
            _const_spec(wout.shape),
            _const_spec(g.shape),
            _const_spec(b.shape),
        ],
        out_specs=pl.BlockSpec((TM, D_MODEL), out_map),
        out_shape=jax.ShapeDtypeStruct((out_rows, D_MODEL), jnp.float32),
        scratch_shapes=[
            pltpu.VMEM((TM, D_FF), jnp.bfloat16),
            pltpu.VMEM((SUBLANES, 2 * D_FF), jnp.float32),
            pltpu.VMEM((SUBLANES, 2 * D_FF), jnp.float32),
        ],
        compiler_params=_params(1),
        name="ffn_ln",
    )(h, win, cw_r, wout, g, b)


def _qkv_kernel(tiles_per_batch, h_ref, wq_ref, wkv_ref, wf_ref, bf_ref,
                q_ref, k_ref, v_ref, qa_ref, ka_ref, carry_ref, meta_c_ref):
    i = pl.program_id(0)

    @pl.when(i == 0)
    def _():
        carry_ref[...] = jnp.zeros_like(carry_ref)

    hb = h_ref[...].astype(jnp.bfloat16)
    z = jnp.dot(hb, wf_ref[...], preferred_element_type=jnp.float32) + bf_ref[...]
    q = jnp.dot(hb, wq_ref[...], preferred_element_type=jnp.float32)
    q_ref[...] = (q * (HEAD_DIM ** -0.5 * LOG2E)).astype(jnp.bfloat16)
    k_ref[...] = jnp.dot(hb, wkv_ref[:, :D_MODEL], preferred_element_type=jnp.float32).astype(jnp.bfloat16)
    v_ref[...] = jnp.dot(hb, wkv_ref[:, D_MODEL:], preferred_element_type=jnp.float32).astype(jnp.bfloat16)

    logf = jnp.minimum(z, 0.0) - jnp.log1p(jnp.exp(-jnp.abs(z)))
    row = lax.broadcasted_iota(jnp.int32, (TM, 1), 0)
    valid = jnp.logical_or(i > 0, row >= META_ROW0)
    x = jnp.where(valid, logf, 0.0)
    s = 1
    while s < TM:
        x = x + jnp.where(row >= s, pltpu.roll(x, s, axis=0), 0.0)
        s *= 2
    batch_start = jnp.logical_and(i > 0, (i - 1) % tiles_per_batch == 0)
    last = SUBLANES - 1
    base = jnp.where(batch_start, meta_c_ref[last:, :], carry_ref[last:, :])
    c = x + base
    tail = c[TM - SUBLANES:, :]
    carry_ref[...] = tail

    @pl.when(i == 0)
    def _():
        meta_c_ref[...] = tail

    c2 = c * LOG2E
    hi = c2.astype(jnp.bfloat16)
    r1 = c2 - hi.astype(jnp.float32)
    mid = r1.astype(jnp.bfloat16)
    lo = (r1 - mid.astype(jnp.float32)).astype(jnp.bfloat16)
    src = lax.broadcasted_iota(jnp.int32, (LANES, LANES), 0)
    dst = lax.broadcasted_iota(jnp.int32, (LANES, LANES), 1)

    def spread(piece, offset):
        sel = jnp.logical_and(dst == AUG * src + offset, src < N_HEADS).astype(jnp.bfloat16)
        return jnp.dot(piece, sel, preferred_element_type=jnp.float32)

    slot = lax.broadcasted_iota(jnp.int32, (1, LANES), 1) % AUG
    ones_q = jnp.logical_and(slot >= 3, slot < 6).astype(jnp.float32)
    ones_k = (slot < 3).astype(jnp.float32)
    qa_ref[...] = (spread(hi, 0) + spread(mid, 1) + spread(lo, 2) + ones_q).astype(jnp.bfloat16)
    ka_ref[...] = (ones_k - (spread(hi, 3) + spread(mid, 4) + spread(lo, 5))).astype(jnp.bfloat16)


def _qkv(h, wq, wkv, wf_pad, bf_pad, tiles_per_batch):
    rows = h.shape[0]
    n_tiles = rows // TM
    tile = pl.BlockSpec((TM, D_MODEL), lambda i: (i, 0))
    aug_tile = pl.BlockSpec((TM, LANES), lambda i: (i, 0))
    wide = jax.ShapeDtypeStruct((rows, D_MODEL), jnp.bfloat16)
    narrow = jax.ShapeDtypeStruct((rows, LANES), jnp.bfloat16)
    return pl.pallas_call(
        functools.partial(_qkv_kernel, tiles_per_batch),
        grid=(n_tiles,),
        in_specs=[tile, _const_spec(wq.shape), _const_spec(wkv.shape),
                  _const_spec(wf_pad.shape), _const_spec(bf_pad.shape)],
        out_specs=[tile, tile, tile, aug_tile, aug_tile],
        out_shape=[wide, wide, wide, narrow, narrow],
        scratch_shapes=[pltpu.VMEM((SUBLANES, LANES), jnp.float32),
                        pltpu.VMEM((SUBLANES, LANES), jnp.float32)],
        compiler_params=_params(1),
        name="qkv_proj",
    )(h, wq, wkv, wf_pad, bf_pad)


def _attention_steps(n_tiles, tiles_per_batch):
    f_tab, kv_tab, first, last, kmin, diag = [], [], [], [], [], []
    for f in range(n_tiles):
        if f == 0:
            kvs = [0]
        else:
            b, qi = divmod(f - 1, tiles_per_batch)
            kvs = [0] + [1 + b * tiles_per_batch + t for t in range(qi + 1)]
        for n, kv in enumerate(kvs):
            f_tab.append(f)
            kv_tab.append(kv)
            first.append(int(n == 0))
            last.append(int(n == len(kvs) - 1))
            is_diag = kv == f
            diag.append(int(is_diag))
            kmin.append(META_ROW0 if kv == 0 else (0 if is_diag else -1))
    return [np.asarray(t, np.int32) for t in (f_tab, kv_tab, first, last, kmin, diag)]


def _attn_kernel(f_tab, kv_tab, first_tab, last_tab, kmin_tab, diag_tab,
                 q_ref, qa_ref, k_ref, ka_ref, v_ref, o_ref, m_ref, acc_ref):
    del f_tab, kv_tab
    step_id = pl.program_id(0)
    kmin = kmin_tab[step_id]

    @pl.when(first_tab[step_id] == 1)
    def _():
        m_ref[...] = jnp.full_like(m_ref, NEG_INF)
        acc_ref[...] = jnp.zeros_like(acc_ref)

    lane = lax.broadcasted_iota(jnp.int32, (1, LANES), 1)
    low_half = lane < HEAD_DIM
    nt_dims = (((1,), (1,)), ((), ()))

    def step(masked, k0):
        tk = TM - k0
        n_kblk = tk // LANES
        if masked:
            qrow = lax.broadcasted_iota(jnp.int32, (TM, tk), 0)
            krow = lax.broadcasted_iota(jnp.int32, (TM, tk), 1) + k0
            causal = diag_tab[step_id] == 1
            keep = jnp.logical_and(krow >= kmin, jnp.logical_or(krow <= qrow, jnp.logical_not(causal)))
        qa = qa_ref[...]
        ka = ka_ref[k0:, :]
        n_pairs = N_HEADS // 2

        def pair_scores(p):
            cols = slice(p * LANES, (p + 1) * LANES)
            qp = q_ref[:, cols]
            k_ext = jnp.concatenate([k_ref[k0:, cols], ka], axis=1)
            vp = v_ref[k0:, cols]
            scores, vms = [], []
            for hh in range(2):
                hd = 2 * p + hh
                in_head = low_half if hh == 0 else jnp.logical_not(low_half)
                ones_lane = HEAD_DIM if hh == 0 else 0
                aug_lanes = jnp.logical_and(lane >= AUG * hd, lane < AUG * (hd + 1))
                q_ext = jnp.concatenate([jnp.where(in_head, qp, jnp.zeros_like(qp)),
                                         jnp.where(aug_lanes, qa, jnp.zeros_like(qa))], axis=1)
                vms.append(jnp.where(in_head, vp, jnp.where(lane == ones_lane, 1.0, 0.0).astype(vp.dtype)))
                scores.append(lax.dot_general(q_ext, k_ext, nt_dims, preferred_element_type=jnp.float32))
            return scores, vms

        def softmax_pv(hd, s, vm):
            m_prev = m_ref[hd]
            if masked and k0 == 0:
                tri = (lax.broadcasted_iota(jnp.int32, (LANES, LANES), 1)
                       <= lax.broadcasted_iota(jnp.int32, (LANES, LANES), 0))
                blk = lambda qb, kb: s[qb * LANES:(qb + 1) * LANES, kb * LANES:(kb + 1) * LANES]
                diag_blk = [jnp.where(tri, blk(b, b), NEG_INF) for b in range(n_kblk)]
                m_rows = []
                for qb in range(n_kblk):
                    m_part = diag_blk[qb]
                    for kb in range(qb):
                        m_part = jnp.maximum(m_part, blk(qb, kb))
                    m_rows.append(jnp.max(m_part, axis=1, keepdims=True))
                m_new = jnp.maximum(m_prev, jnp.concatenate(m_rows, axis=0))
                p_rows = []
                for qb in range(n_kblk):
                    mq = m_new[qb * LANES:(qb + 1) * LANES, :]
                    p_rows.append(jnp.concatenate(
                        [jnp.exp2(blk(qb, kb) - mq) for kb in range(qb)]
                        + [jnp.exp2(diag_blk[qb] - mq)]
                        + [jnp.zeros((LANES, LANES), jnp.float32)] * (n_kblk - 1 - qb), axis=1))
                pe = jnp.concatenate(p_rows, axis=0).astype(jnp.bfloat16)
            else:
                if masked:
                    s = jnp.where(keep, s, NEG_INF)
                m_part = s[:, :LANES]
                for kb in range(1, n_kblk):
                    m_part = jnp.maximum(m_part, s[:, kb * LANES:(kb + 1) * LANES])
                m_new = jnp.maximum(m_prev, jnp.max(m_part, axis=1, keepdims=True))
                pe = jnp.concatenate([jnp.exp2(s[:, kb * LANES:(kb + 1) * LANES] - m_new)
                                      for kb in range(n_kblk)], axis=1).astype(jnp.bfloat16)
            alpha = jnp.exp2(m_prev - m_new)
            pv = jnp.dot(pe, vm, preferred_element_type=jnp.float32)
            acc_ref[hd] = alpha * acc_ref[hd] + pv
            m_ref[hd] = m_new

        cur = pair_scores(0)
        for p in range(n_pairs):
            nxt = pair_scores(p + 1) if p + 1 < n_pairs else None
            for hh in range(2):
                softmax_pv(2 * p + hh, cur[0][hh], cur[1][hh])
            cur = nxt

    @pl.when(kmin > 0)
    def _():
        step(True, TM - LANES)

    @pl.when(kmin == 0)
    def _():
        step(True, 0)

    @pl.when(kmin < 0)
    def _():
        step(False, 0)

    @pl.when(last_tab[step_id] == 1)
    def _():
        for p in range(N_HEADS // 2):
            cols = slice(p * LANES, (p + 1) * LANES)
            a0 = acc_ref[2 * p]
            a1 = acc_ref[2 * p + 1]
            l0 = a0[:, HEAD_DIM:HEAD_DIM + 1]
            l1 = a1[:, 0:1]
            o_ref[:, cols] = jnp.where(low_half, a0 / l0, a1 / l1).astype(o_ref.dtype)


def _attention(q, qa, k, ka, v, tiles_per_batch):
    rows = q.shape[0]
    tabs = _attention_steps(rows // TM, tiles_per_batch)
    q_map = lambda s, f_tab, kv_tab, *_: (f_tab[s], 0)
    kv_map = lambda s, f_tab, kv_tab, *_: (kv_tab[s], 0)
    grid_spec = pltpu.PrefetchScalarGridSpec(
        num_scalar_prefetch=len(tabs),
        grid=(len(tabs[0]),),
        in_specs=[
            pl.BlockSpec((TM, D_MODEL), q_map),
            pl.BlockSpec((TM, LANES), q_map),
            pl.BlockSpec((TM, D_MODEL), kv_map),
            pl.BlockSpec((TM, LANES), kv_map),
            pl.BlockSpec((TM, D_MODEL), kv_map),
        ],
        out_specs=pl.BlockSpec((TM, D_MODEL), q_map),
        scratch_shapes=[
            pltpu.VMEM((N_HEADS, TM, LANES), jnp.float32),
            pltpu.VMEM((N_HEADS, TM, LANES), jnp.float32),
        ],
    )
    return pl.pallas_call(
        _attn_kernel,
        grid_spec=grid_spec,
        out_shape=jax.ShapeDtypeStruct((rows, D_MODEL), jnp.bfloat16),
        compiler_params=_params(1),
        name="fox_attention",
    )(*[jnp.asarray(t) for t in tabs], q, qa, k, ka, v)


def _oproj_ln_kernel(h_ref, o_ref, wo_ref, g_ref, b_ref, out_ref):
    mix = jnp.dot(o_ref[...], wo_ref[...], preferred_element_type=jnp.float32)
    out_ref[...] = _layer_norm(ALPHA * h_ref[...] + mix, g_ref[...], b_ref[...])


def _oproj_ln(h, o, wo, g, b):
    rows = h.shape[0]
    tile = pl.BlockSpec((TM, D_MODEL), lambda i: (i, 0))
    return pl.pallas_call(
        _oproj_ln_kernel,
        grid=(rows // TM,),
        in_specs=[tile, tile, _const_spec(wo.shape), _const_spec(g.shape), _const_spec(b.shape)],
        out_specs=tile,
        out_shape=jax.ShapeDtypeStruct((rows, D_MODEL), jnp.float32),
        compiler_params=_params(1),
        name="oproj_ln",
    )(h, o, wo, g, b)


def _ffn_weights(w_in, conv_w, conv_b, w_out):
    F, FC, NC = D_FF, FFN_CHUNK, N_FFN_CHUNKS

    def regroup(m):
        a = m[..., :F].reshape(m.shape[:-1] + (NC, FC))
        g = m[..., F:].reshape(m.shape[:-1] + (NC, FC))
        return jnp.moveaxis(jnp.concatenate([a, g], axis=-1), -2, 0)

    win = w_in.astype(jnp.bfloat16)
    taps = jnp.concatenate([conv_w, conv_b[None, :],
                            jnp.zeros((SUBLANES - CONV_WIDTH - 1, 2 * F), conv_w.dtype)], axis=0)
    cw_r = regroup(taps)
    return win, cw_r, w_out.astype(jnp.bfloat16)


def kernel(x, meta, pool_w, pool_scale, w_kv, w_f, b_f, w_q, w_o, ffn_w_in, ffn_conv_w, ffn_conv_b,
           ffn_w_out, ln_g, ln_b):
    B, S, D = x.shape
    assert D == D_MODEL and S % TM == 0 and meta.shape == (N_META, D_MODEL)
    assert pool_w.shape[0] == N_A_LAYERS == 1 and w_q.shape[0] == DEPTH - N_A_LAYERS == 1
    tpb = S // TM

    meta_tile = jnp.concatenate([jnp.zeros((META_ROW0, D), x.dtype), meta.astype(x.dtype)], axis=0)
    row2 = lambda a: a.reshape(1, -1)

    h = _pool_ln(x.reshape(B * S, D), meta_tile, pool_w[0].astype(jnp.bfloat16), row2(pool_scale[0]),
                 row2(ln_g[0, 0]), row2(ln_b[0, 0]), tpb)
    h = _ffn_ln(h, *_ffn_weights(ffn_w_in[0], ffn_conv_w[0], ffn_conv_b[0], ffn_w_out[0]),
                row2(ln_g[0, 1]), row2(ln_b[0, 1]), tpb, drop_meta=False)

    wf_pad = jnp.pad(w_f, ((0, 0), (0, LANES - N_HEADS))).astype(jnp.bfloat16)
    bf_pad = jnp.pad(b_f, (0, LANES - N_HEADS)).reshape(1, LANES)
    q, k, v, qa, ka = _qkv(h, w_q[0].astype(jnp.bfloat16), w_kv.astype(jnp.bfloat16), wf_pad, bf_pad, tpb)
    o = _attention(q, qa, k, ka, v, tpb)
    h = _oproj_ln(h, o, w_o[0].astype(jnp.bfloat16), row2(ln_g[1, 0]), row2(ln_b[1, 0]))
    h = _ffn_ln(h, *_ffn_weights(ffn_w_in[1], ffn_conv_w[1], ffn_conv_b[1], ffn_w_out[1]),
                row2(ln_g[1, 1]), row2(ln_b[1, 1]), tpb, drop_meta=True)
    return h.reshape(B, S, D)
```

```python
import functools
import math

import numpy as np
import jax
import jax.numpy as jnp
from jax import lax
from jax.experimental import pallas as pl
from jax.experimental.pallas import tpu as pltpu

D_MODEL = 1024
DEPTH = 2
N_META = 16
POOL_WINDOWS = (2, 4, 8, 16)
N_POOL_GROUPS = len(POOL_WINDOWS)
POOL_GROUP_DIM = D_MODEL // N_POOL_GROUPS
HEAD_DIM = 64
N_HEADS = D_MODEL // HEAD_DIM
D_FF = ((8 * D_MODEL // 3 + 127) // 128) * 128
CONV_WIDTH = 3
N_A_LAYERS = DEPTH // 2
ALPHA = (2.0 * DEPTH) ** 0.25
LN_EPS = 1e-5
NEG_INF = -1e30
LOG2E = math.log2(math.e)

TM = 512
META_ROW0 = TM - N_META
HALO = 16
FFN_CHUNK = 256
N_FFN_CHUNKS = D_FF // FFN_CHUNK
LANES = 128
SUBLANES = 8
AUG = LANES // N_HEADS
VMEM_LIMIT_BYTES = 56 * 1024 * 1024

assert D_FF % FFN_CHUNK == 0 and AUG >= 6


def _layer_norm(y, g, b):
    mu = jnp.mean(y, axis=-1, keepdims=True)
    xc = y - mu
    var = jnp.mean(xc * xc, axis=-1, keepdims=True)
    return xc * lax.rsqrt(var + LN_EPS) * g + b


def _const_spec(shape):
    nd = len(shape)
    return pl.BlockSpec(shape, lambda *_: (0,) * nd, pipeline_mode=pl.Buffered(1))


def _params(n_axes):
    return pltpu.CompilerParams(dimension_semantics=("arbitrary",) * n_axes,
                                vmem_limit_bytes=VMEM_LIMIT_BYTES)


def _pool_ln_kernel(tiles_per_batch, x_ref, xhalo_ref, meta_ref, w_ref, scale_ref, g_ref, b_ref, o_ref):
    i = pl.program_id(0)
    is_meta = i == 0
    meta_tile = meta_ref[...]
    h = jnp.where(is_meta, meta_tile, x_ref[...])
    batch_start = (i - 1) % tiles_per_batch == 0
    halo = jnp.where(batch_start, meta_tile[TM - HALO:, :], xhalo_ref[...])
    halo = jnp.where(is_meta, 0.0, halo)
    row = lax.broadcasted_iota(jnp.int32, (TM, 1), 0)
    t = jnp.where(is_meta, row - (META_ROW0 - 1), HALO)
    outs = []
    G = POOL_GROUP_DIM
    for gi, w in enumerate(POOL_WINDOWS):
        sl = slice(gi * G, (gi + 1) * G)
        hg = h[:, sl]
        s = jnp.concatenate([halo[:, sl], hg], axis=0)
        k = 1
        while k < w:
            s = s + pltpu.roll(s, k, axis=0)
            k *= 2
        count = jnp.clip(t, 1, w).astype(jnp.float32)
        pooled = s[HALO:, :] / count
        diff = (pooled - hg).astype(jnp.bfloat16)
        mixed = jnp.dot(diff, w_ref[gi], preferred_element_type=jnp.float32)
        outs.append(ALPHA * hg + mixed * scale_ref[:, sl])
    y = jnp.concatenate(outs, axis=1)
    o_ref[...] = _layer_norm(y, g_ref[...], b_ref[...])


def _pool_ln(x_flat, meta_tile, w_bf16, scale, g, b, tiles_per_batch):
    n_tiles = x_flat.shape[0] // TM + 1
    rows = n_tiles * TM
    halo_per_tile = TM // HALO

    def halo_map(i):
        blk = jnp.maximum((i - 1) * halo_per_tile - 1, 0)
        return (jnp.where((i - 1) % tiles_per_batch == 0, 0, blk), 0)

    return pl.pallas_call(
        functools.partial(_pool_ln_kernel, tiles_per_batch),
        grid=(n_tiles,),
        in_specs=[
            pl.BlockSpec((TM, D_MODEL), lambda i: (jnp.maximum(i - 1, 0), 0)),
            pl.BlockSpec((HALO, D_MODEL), halo_map),
            _const_spec(meta_tile.shape),
            _const_spec(w_bf16.shape),
            _const_spec(scale.shape),
            _const_spec(g.shape),
            _const_spec(b.shape),
        ],
        out_specs=pl.BlockSpec((TM, D_MODEL), lambda i: (i, 0)),
        out_shape=jax.ShapeDtypeStruct((rows, D_MODEL), jnp.float32),
        compiler_params=_params(1),
        name="pool_ln",
    )(x_flat, x_flat, meta_tile, w_bf16, scale, g, b)


def _ffn_kernel(tiles_per_batch, h_ref, win_ref, cw_ref, wout_ref, g_ref, b_ref, o_ref,
                act_ref, carry_ref, meta_tail_ref):
    i = pl.program_id(0)

    @pl.when(i == 0)
    def _():
        carry_ref[...] = jnp.zeros_like(carry_ref)
        meta_tail_ref[...] = jnp.zeros_like(meta_tail_ref)

    h = h_ref[...]
    row = lax.broadcasted_iota(jnp.int32, (TM, 1), 0)
    valid = jnp.logical_or(i > 0, row >= META_ROW0)
    hb = jnp.where(valid, h, 0.0).astype(jnp.bfloat16)
    batch_start = jnp.logical_and(i > 0, (i - 1) % tiles_per_batch == 0)
    FC = FFN_CHUNK
    for c in range(N_FFN_CHUNKS):
        cols = slice(c * 2 * FC, (c + 1) * 2 * FC)
        u = jnp.concatenate(
            [jnp.dot(hb, win_ref[:, c * FC:(c + 1) * FC], preferred_element_type=jnp.float32),
             jnp.dot(hb, win_ref[:, D_FF + c * FC:D_FF + (c + 1) * FC], preferred_element_type=jnp.float32)],
            axis=1)
        prev = jnp.where(batch_start, meta_tail_ref[:, cols], carry_ref[:, cols])
        p1 = prev[SUBLANES - 1:SUBLANES, :]
        p2 = prev[SUBLANES - 2:SUBLANES - 1, :]
        u1 = jnp.where(row == 0, p1, pltpu.roll(u, 1, axis=0))
        u2 = jnp.where(row == 0, p2, jnp.where(row == 1, p1, pltpu.roll(u, 2, axis=0)))
        cw = cw_ref[c]
        cv = cw[3:4, :] + cw[0:1, :] * u2 + cw[1:2, :] * u1 + cw[2:3, :] * u
        tail = u[TM - SUBLANES:, :]
        carry_ref[:, cols] = tail
        meta_tail_ref[:, cols] = jnp.where(i == 0, tail, meta_tail_ref[:, cols])

        a = cv[:, :FC]
        gate = cv[:, FC:]
        act_ref[:, c * FC:(c + 1) * FC] = (a * jax.nn.sigmoid(a) * gate).astype(jnp.bfloat16)
    ffn = jnp.dot(act_ref[...], wout_ref[...], preferred_element_type=jnp.float32)
    o_ref[...] = _layer_norm(ALPHA * h + ffn, g_ref[...], b_ref[...])


def _ffn_ln(h, win, cw_r, wout, g, b, tiles_per_batch, drop_meta):
    rows = h.shape[0]
    n_tiles = rows // TM
    if drop_meta:
        out_rows, out_map = rows - TM, (lambda i: (jnp.maximum(i - 1, 0), 0))
    else:
        out_rows, out_map = rows, (lambda i: (i, 0))
    return pl.pallas_call(
        functools.partial(_ffn_kernel, tiles_per_batch),
        grid=(n_tiles,),
        in_specs=[
            pl.BlockSpec((TM, D_MODEL), lambda i: (i, 0)),
            _const_spec(win.shape),
            _const_spec(cw_r.shape),
            _const_spec(wout.shape),
            _const_spec(g.shape),
            _const_spec(b.shape),
        ],
        out_specs=pl.BlockSpec((TM, D_MODEL), out_map),
        out_shape=jax.ShapeDtypeStruct((out_rows, D_MODEL), jnp.float32),
        scratch_shapes=[
            pltpu.VMEM((TM, D_FF), jnp.bfloat16),
            pltpu.VMEM((SUBLANES, 2 * D_FF), jnp.float32),
            pltpu.VMEM((SUBLANES, 2 * D_FF), jnp.float32),
        ],
        compiler_params=_params(1),
        name="ffn_ln",
    )(h, win, cw_r, wout, g, b)


def _qkv_kernel(tiles_per_batch, h_ref, wq_ref, wkv_ref, wf_ref, bf_ref,
                q_ref, k_ref, v_ref, qa_ref, ka_ref, carry_ref, meta_c_ref):
    i = pl.program_id(0)

    @pl.when(i == 0)
    def _():
        carry_ref[...] = jnp.zeros_like(carry_ref)

    hb = h_ref[...].astype(jnp.bfloat16)
    z = jnp.dot(hb, wf_ref[...], preferred_element_type=jnp.float32) + bf_ref[...]
    q = jnp.dot(hb, wq_ref[...], preferred_element_type=jnp.float32)
    q_ref[...] = (q * (HEAD_DIM ** -0.5 * LOG2E)).astype(jnp.bfloat16)
    k_ref[...] = jnp.dot(hb, wkv_ref[:, :D_MODEL], preferred_element_type=jnp.float32).astype(jnp.bfloat16)
    v_ref[...] = jnp.dot(hb, wkv_ref[:, D_MODEL:], preferred_element_type=jnp.float32).astype(jnp.bfloat16)

    logf = jnp.minimum(z, 0.0) - jnp.log1p(jnp.exp(-jnp.abs(z)))
    row = lax.broadcasted_iota(jnp.int32, (TM, 1), 0)
    valid = jnp.logical_or(i > 0, row >= META_ROW0)
    x = jnp.where(valid, logf, 0.0)
    s = 1
    while s < TM:
        x = x + jnp.where(row >= s, pltpu.roll(x, s, axis=0), 0.0)
        s *= 2
    batch_start = jnp.logical_and(i > 0, (i - 1) % tiles_per_batch == 0)
    last = SUBLANES - 1
    base = jnp.where(batch_start, meta_c_ref[last:, :], carry_ref[last:, :])
    c = x + base
    tail = c[TM - SUBLANES:, :]
    carry_ref[...] = tail

    @pl.when(i == 0)
    def _():
        meta_c_ref[...] = tail

    c2 = c * LOG2E
    hi = c2.astype(jnp.bfloat16)
    r1 = c2 - hi.astype(jnp.float32)
    mid = r1.astype(jnp.bfloat16)
    lo = (r1 - mid.astype(jnp.float32)).astype(jnp.bfloat16)
    src = lax.broadcasted_iota(jnp.int32, (LANES, LANES), 0)
    dst = lax.broadcasted_iota(jnp.int32, (LANES, LANES), 1)

    def spread(piece, offset):
        sel = jnp.logical_and(dst == AUG * src + offset, src < N_HEADS).astype(jnp.bfloat16)
        return jnp.dot(piece, sel, preferred_element_type=jnp.float32)

    slot = lax.broadcasted_iota(jnp.int32, (1, LANES), 1) % AUG
    ones_q = jnp.logical_and(slot >= 3, slot < 6).astype(jnp.float32)
    ones_k = (slot < 3).astype(jnp.float32)
    qa_ref[...] = (spread(hi, 0) + spread(mid, 1) + spread(lo, 2) + ones_q).astype(jnp.bfloat16)
    ka_ref[...] = (ones_k - (spread(hi, 3) + spread(mid, 4) + spread(lo, 5))).astype(jnp.bfloat16)


def _qkv(h, wq, wkv, wf_pad, bf_pad, tiles_per_batch):
    rows = h.shape[0]
    n_tiles = rows // TM
    tile = pl.BlockSpec((TM, D_MODEL), lambda i: (i, 0))
    aug_tile = pl.BlockSpec((TM, LANES), lambda i: (i, 0))
    kv_pos = lambda i: (jnp.where(i == 0, n_tiles - 1, i - 1), 0)
    kv_tile = pl.BlockSpec((TM, D_MODEL), kv_pos)
    kv_aug_tile = pl.BlockSpec((TM, LANES), kv_pos)
    wide = jax.ShapeDtypeStruct((rows, D_MODEL), jnp.bfloat16)
    narrow = jax.ShapeDtypeStruct((rows, LANES), jnp.bfloat16)
    return pl.pallas_call(
        functools.partial(_qkv_kernel, tiles_per_batch),
        grid=(n_tiles,),
        in_specs=[tile, _const_spec(wq.shape), _const_spec(wkv.shape),
                  _const_spec(wf_pad.shape), _const_spec(bf_pad.shape)],
        out_specs=[tile, kv_tile, kv_tile, aug_tile, kv_aug_tile],
        out_shape=[wide, wide, wide, narrow, narrow],
        scratch_shapes=[pltpu.VMEM((SUBLANES, LANES), jnp.float32),
                        pltpu.VMEM((SUBLANES, LANES), jnp.float32)],
        compiler_params=_params(1),
        name="qkv_proj",
    )(h, wq, wkv, wf_pad, bf_pad)


KV_GROUP = 2
MODE_META, MODE_FULL, MODE_TAIL1, MODE_TAIL2 = 0, 1, 2, 3


def _attention_steps(n_tiles, tiles_per_batch):
    f_tab, kv_tab, first, last, mode, diag = [], [], [], [], [], []
    meta_block = (n_tiles - 1) // KV_GROUP
    for f in range(n_tiles):
        steps = [(meta_block, MODE_META)]
        if f > 0:
            b, qi = divmod(f - 1, tiles_per_batch)
            base = b * tiles_per_batch // KV_GROUP
            steps += [(base + j, MODE_FULL) for j in range(qi // 2)]
            steps.append((base + qi // 2, MODE_TAIL1 if qi % 2 == 0 else MODE_TAIL2))
        for n, (kv, md) in enumerate(steps):
            f_tab.append(f)
            kv_tab.append(kv)
            first.append(int(n == 0))
            last.append(int(n == len(steps) - 1))
            mode.append(md)
            diag.append(int(f == 0))
    return [np.asarray(t, np.int32) for t in (f_tab, kv_tab, first, last, mode, diag)]


def _attn_kernel(n_tiles, f_tab, kv_tab, first_tab, last_tab, mode_tab, diag_tab,
                 q_ref, qa_ref, k_ref, ka_ref, v_ref, o_ref, m_ref, acc_ref):
    del f_tab, kv_tab
    step_id = pl.program_id(0)
    mode = mode_tab[step_id]

    @pl.when(first_tab[step_id] == 1)
    def _():
        m_ref[...] = jnp.full_like(m_ref, NEG_INF)
        acc_ref[...] = jnp.zeros_like(acc_ref)

    lane = lax.broadcasted_iota(jnp.int32, (1, LANES), 1)
    low_half = lane < HEAD_DIM
    nt_dims = (((1,), (1,)), ((), ()))

    def step(subs):
        n_pairs = N_HEADS // 2
        qa = qa_ref[...]
        k0_of = {"meta": TM - LANES, "full": 0, "diag": 0}
        if any(kind == "meta" for _, kind in subs):
            qrow = lax.broadcasted_iota(jnp.int32, (TM, LANES), 0)
            krow = lax.broadcasted_iota(jnp.int32, (TM, LANES), 1) + (TM - LANES)
            causal = diag_tab[step_id] == 1
            keep = jnp.logical_and(krow >= META_ROW0, jnp.logical_or(krow <= qrow, jnp.logical_not(causal)))

        def pair_scores(item):
            sub, kind, p = item
            r0, r1 = sub * TM + k0_of[kind], (sub + 1) * TM
            cols = slice(p * LANES, (p + 1) * LANES)
            qp = q_ref[:, cols]
            k_ext = jnp.concatenate([k_ref[r0:r1, cols], ka_ref[r0:r1, :]], axis=1)
            vp = v_ref[r0:r1, cols]
            scores, vms = [], []
            for hh in range(2):
                hd = 2 * p + hh
                in_head = low_half if hh == 0 else jnp.logical_not(low_half)
                ones_lane = HEAD_DIM if hh == 0 else 0
                aug_lanes = jnp.logical_and(lane >= AUG * hd, lane < AUG * (hd + 1))
                q_ext = jnp.concatenate([jnp.where(in_head, qp, jnp.zeros_like(qp)),
                                         jnp.where(aug_lanes, qa, jnp.zeros_like(qa))], axis=1)
                vms.append(jnp.where(in_head, vp, jnp.where(lane == ones_lane, 1.0, 0.0).astype(vp.dtype)))
                scores.append(lax.dot_general(q_ext, k_ext, nt_dims, preferred_element_type=jnp.float32))
            return scores, vms

        def softmax_pv(kind, hd, s, vm):
            n_kblk = s.shape[1] // LANES
            m_prev = m_ref[hd]
            if kind == "diag":
                tri = (lax.broadcasted_iota(jnp.int32, (LANES, LANES), 1)
                       <= lax.broadcasted_iota(jnp.int32, (LANES, LANES), 0))
                blk = lambda qb, kb: s[qb * LANES:(qb + 1) * LANES, kb * LANES:(kb + 1) * LANES]
                diag_blk = [jnp.where(tri, blk(b, b), NEG_INF) for b in range(n_kblk)]
                m_rows = []
                for qb in range(n_kblk):
                    m_part = diag_blk[qb]
                    for kb in range(qb):
                        m_part = jnp.maximum(m_part, blk(qb, kb))
                    m_rows.append(jnp.max(m_part, axis=1, keepdims=True))
                m_new = jnp.maximum(m_prev, jnp.concatenate(m_rows, axis=0))
                p_rows = []
                for qb in range(n_kblk):
                    mq = m_new[qb * LANES:(qb + 1) * LANES, :]
                    p_rows.append(jnp.concatenate(
                        [jnp.exp2(blk(qb, kb) - mq) for kb in range(qb)]
                        + [jnp.exp2(diag_blk[qb] - mq)]
                        + [jnp.zeros((LANES, LANES), jnp.float32)] * (n_kblk - 1 - qb), axis=1))
                pe = jnp.concatenate(p_rows, axis=0).astype(jnp.bfloat16)
            else:
                if kind == "meta":
                    s = jnp.where(keep, s, NEG_INF)
                m_part = s[:, :LANES]
                for kb in range(1, n_kblk):
                    m_part = jnp.maximum(m_part, s[:, kb * LANES:(kb + 1) * LANES])
                m_new = jnp.maximum(m_prev, jnp.max(m_part, axis=1, keepdims=True))
                pe = jnp.concatenate([jnp.exp2(s[:, kb * LANES:(kb + 1) * LANES] - m_new)
                                      for kb in range(n_kblk)], axis=1).astype(jnp.bfloat16)
            alpha = jnp.exp2(m_prev - m_new)
            pv = jnp.dot(pe, vm, preferred_element_type=jnp.float32)
            acc_ref[hd] = alpha * acc_ref[hd] + pv
            m_ref[hd] = m_new

        work = [(sub, kind, p) for sub, kind in subs for p in range(n_pairs)]
        cur = pair_scores(work[0])
        for n, (sub, kind, p) in enumerate(work):
            nxt = pair_scores(work[n + 1]) if n + 1 < len(work) else None
            for hh in range(2):
                softmax_pv(kind, 2 * p + hh, cur[0][hh], cur[1][hh])
            cur = nxt

    @pl.when(mode == MODE_META)
    def _():
        step([((n_tiles - 1) % KV_GROUP, "meta")])

    @pl.when(mode == MODE_FULL)
    def _():
        step([(0, "full"), (1, "full")])

    @pl.when(mode == MODE_TAIL1)
    def _():
        step([(0, "diag")])

    @pl.when(mode == MODE_TAIL2)
    def _():
        step([(1, "diag"), (0, "full")])

    @pl.when(last_tab[step_id] == 1)
    def _():
        for p in range(N_HEADS // 2):
            cols = slice(p * LANES, (p + 1) * LANES)
            a0 = acc_ref[2 * p]
            a1 = acc_ref[2 * p + 1]
            l0 = a0[:, HEAD_DIM:HEAD_DIM + 1]
            l1 = a1[:, 0:1]
            o_ref[:, cols] = jnp.where(low_half, a0 / l0, a1 / l1).astype(o_ref.dtype)


def _attention(q, qa, k, ka, v, tiles_per_batch):
    rows = q.shape[0]
    assert tiles_per_batch % KV_GROUP == 0
    tabs = _attention_steps(rows // TM, tiles_per_batch)
    q_map = lambda s, f_tab, kv_tab, *_: (f_tab[s], 0)
    kv_map = lambda s, f_tab, kv_tab, *_: (kv_tab[s], 0)
    grid_spec = pltpu.PrefetchScalarGridSpec(
        num_scalar_prefetch=len(tabs),
        grid=(len(tabs[0]),),
        in_specs=[
            pl.BlockSpec((TM, D_MODEL), q_map),
            pl.BlockSpec((TM, LANES), q_map),
            pl.BlockSpec((KV_GROUP * TM, D_MODEL), kv_map),
            pl.BlockSpec((KV_GROUP * TM, LANES), kv_map),
            pl.BlockSpec((KV_GROUP * TM, D_MODEL), kv_map),
        ],
        out_specs=pl.BlockSpec((TM, D_MODEL), q_map),
        scratch_shapes=[
            pltpu.VMEM((N_HEADS, TM, LANES), jnp.float32),
            pltpu.VMEM((N_HEADS, TM, LANES), jnp.float32),
        ],
    )
    return pl.pallas_call(
        functools.partial(_attn_kernel, rows // TM),
        grid_spec=grid_spec,
        out_shape=jax.ShapeDtypeStruct((rows, D_MODEL), jnp.bfloat16),
        compiler_params=_params(1),
        name="fox_attention",
    )(*[jnp.asarray(t) for t in tabs], q, qa, k, ka, v)


def _oproj_ln_kernel(h_ref, o_ref, wo_ref, g_ref, b_ref, out_ref):
    mix = jnp.dot(o_ref[...], wo_ref[...], preferred_element_type=jnp.float32)
    out_ref[...] = _layer_norm(ALPHA * h_ref[...] + mix, g_ref[...], b_ref[...])


def _oproj_ln(h, o, wo, g, b):
    rows = h.shape[0]
    tile = pl.BlockSpec((TM, D_MODEL), lambda i: (i, 0))
    return pl.pallas_call(
        _oproj_ln_kernel,
        grid=(rows // TM,),
        in_specs=[tile, tile, _const_spec(wo.shape), _const_spec(g.shape), _const_spec(b.shape)],
        out_specs=tile,
        out_shape=jax.ShapeDtypeStruct((rows, D_MODEL), jnp.float32),
        compiler_params=_params(1),
        name="oproj_ln",
    )(h, o, wo, g, b)


def _ffn_weights(w_in, conv_w, conv_b, w_out):
    F, FC, NC = D_FF, FFN_CHUNK, N_FFN_CHUNKS

    def regroup(m):
        a = m[..., :F].reshape(m.shape[:-1] + (NC, FC))
        g = m[..., F:].reshape(m.shape[:-1] + (NC, FC))
        return jnp.moveaxis(jnp.concatenate([a, g], axis=-1), -2, 0)

    win = w_in.astype(jnp.bfloat16)
    taps = jnp.concatenate([conv_w, conv_b[None, :],
                            jnp.zeros((SUBLANES - CONV_WIDTH - 1, 2 * F), conv_w.dtype)], axis=0)
    cw_r = regroup(taps)
    return win, cw_r, w_out.astype(jnp.bfloat16)


def kernel(x, meta, pool_w, pool_scale, w_kv, w_f, b_f, w_q, w_o, ffn_w_in, ffn_conv_w, ffn_conv_b,
           ffn_w_out, ln_g, ln_b):
    B, S, D = x.shape
    assert D == D_MODEL and S % TM == 0 and meta.shape == (N_META, D_MODEL)
    assert pool_w.shape[0] == N_A_LAYERS == 1 and w_q.shape[0] == DEPTH - N_A_LAYERS == 1
    tpb = S // TM

    meta_tile = jnp.concatenate([jnp.zeros((META_ROW0, D), x.dtype), meta.astype(x.dtype)], axis=0)
    row2 = lambda a: a.reshape(1, -1)

    h = _pool_ln(x.reshape(B * S, D), meta_tile, pool_w[0].astype(jnp.bfloat16), row2(pool_scale[0]),
                 row2(ln_g[0, 0]), row2(ln_b[0, 0]), tpb)
    h = _ffn_ln(h, *_ffn_weights(ffn_w_in[0], ffn_conv_w[0], ffn_conv_b[0], ffn_w_out[0]),
                row2(ln_g[0, 1]), row2(ln_b[0, 1]), tpb, drop_meta=False)

    wf_pad = jnp.pad(w_f, ((0, 0), (0, LANES - N_HEADS))).astype(jnp.bfloat16)
    bf_pad = jnp.pad(b_f, (0, LANES - N_HEADS)).reshape(1, LANES)
    q, k, v, qa, ka = _qkv(h, w_q[0].astype(jnp.bfloat16), w_kv.astype(jnp.bfloat16), wf_pad, bf_pad, tpb)
    o = _attention(q, qa, k, ka, v, tpb)
    h = _oproj_ln(h, o, w_o[0].astype(jnp.bfloat16), row2(ln_g[1, 0]), row2(ln_b[1, 0]))
    h = _ffn_ln(h, *_ffn_weights(ffn_w_in[1], ffn_conv_w[1], ffn_conv_b[1], ffn_w_out[1]),
                row2(ln_g[1, 1]), row2(ln_b[1, 1]), tpb, drop_meta=True)
    return h.reshape(B, S, D)
```

```python
import functools
import math

import numpy as np
import jax
import jax.numpy as jnp
from jax import lax
from jax.experimental import pallas as pl
from jax.experimental.pallas import tpu as pltpu

D_MODEL = 1024
DEPTH = 2
N_META = 16
POOL_WINDOWS = (2, 4, 8, 16)
N_POOL_GROUPS = len(POOL_WINDOWS)
POOL_GROUP_DIM = D_MODEL // N_POOL_GROUPS
HEAD_DIM = 64
N_HEADS = D_MODEL // HEAD_DIM
D_FF = ((8 * D_MODEL // 3 + 127) // 128) * 128
CONV_WIDTH = 3
N_A_LAYERS = DEPTH // 2
ALPHA = (2.0 * DEPTH) ** 0.25
LN_EPS = 1e-5
NEG_INF = -1e30
LOG2E = math.log2(math.e)

TM = 512
META_ROW0 = TM - N_META
HALO = 16
FFN_CHUNK = 256
N_FFN_CHUNKS = D_FF // FFN_CHUNK
LANES = 128
SUBLANES = 8
AUG = LANES // N_HEADS
VMEM_LIMIT_BYTES = 56 * 1024 * 1024

assert D_FF % FFN_CHUNK == 0 and AUG >= 6


def _layer_norm(y, g, b):
    mu = jnp.mean(y, axis=-1, keepdims=True)
    xc = y - mu
    var = jnp.mean(xc * xc, axis=-1, keepdims=True)
    return xc * lax.rsqrt(var + LN_EPS) * g + b


def _const_spec(shape):
    nd = len(shape)
    return pl.BlockSpec(shape, lambda *_: (0,) * nd, pipeline_mode=pl.Buffered(1))


def _params(n_axes):
    return pltpu.CompilerParams(dimension_semantics=("arbitrary",) * n_axes,
                                vmem_limit_bytes=VMEM_LIMIT_BYTES)


def _pool_ln_kernel(tiles_per_batch, x_ref, xhalo_ref, meta_ref, w_ref, scale_ref, g_ref, b_ref, o_ref):
    i = pl.program_id(0)
    is_meta = i == 0
    meta_tile = meta_ref[...]
    h = jnp.where(is_meta, meta_tile, x_ref[...])
    batch_start = (i - 1) % tiles_per_batch == 0
    halo = jnp.where(batch_start, meta_tile[TM - HALO:, :], xhalo_ref[...])
    halo = jnp.where(is_meta, 0.0, halo)
    row = lax.broadcasted_iota(jnp.int32, (TM, 1), 0)
    t = jnp.where(is_meta, row - (META_ROW0 - 1), HALO)
    outs = []
    G = POOL_GROUP_DIM
    for gi, w in enumerate(POOL_WINDOWS):
        sl = slice(gi * G, (gi + 1) * G)
        hg = h[:, sl]
        s = jnp.concatenate([halo[:, sl], hg], axis=0)
        k = 1
        while k < w:
            s = s + pltpu.roll(s, k, axis=0)
            k *= 2
        count = jnp.clip(t, 1, w).astype(jnp.float32)
        pooled = s[HALO:, :] / count
        diff = (pooled - hg).astype(jnp.bfloat16)
        mixed = jnp.dot(diff, w_ref[gi], preferred_element_type=jnp.float32)
        outs.append(ALPHA * hg + mixed * scale_ref[:, sl])
    y = jnp.concatenate(outs, axis=1)
    o_ref[...] = _layer_norm(y, g_ref[...], b_ref[...])


def _pool_ln(x_flat, meta_tile, w_bf16, scale, g, b, tiles_per_batch):
    n_tiles = x_flat.shape[0] // TM + 1
    rows = n_tiles * TM
    halo_per_tile = TM // HALO

    def halo_map(i):
        blk = jnp.maximum((i - 1) * halo_per_tile - 1, 0)
        return (jnp.where((i - 1) % tiles_per_batch == 0, 0, blk), 0)

    return pl.pallas_call(
        functools.partial(_pool_ln_kernel, tiles_per_batch),
        grid=(n_tiles,),
        in_specs=[
            pl.BlockSpec((TM, D_MODEL), lambda i: (jnp.maximum(i - 1, 0), 0)),
            pl.BlockSpec((HALO, D_MODEL), halo_map),
            _const_spec(meta_tile.shape),
            _const_spec(w_bf16.shape),
            _const_spec(scale.shape),
            _const_spec(g.shape),
            _const_spec(b.shape),
        ],
        out_specs=pl.BlockSpec((TM, D_MODEL), lambda i: (i, 0)),
        out_shape=jax.ShapeDtypeStruct((rows, D_MODEL), jnp.float32),
        compiler_params=_params(1),
        name="pool_ln",
    )(x_flat, x_flat, meta_tile, w_bf16, scale, g, b)


def _ffn_kernel(tiles_per_batch, h_ref, win_ref, cw_ref, wout_ref, g_ref, b_ref, o_ref,
                act_ref, carry_ref, meta_tail_ref):
    i = pl.program_id(0)

    @pl.when(i == 0)
    def _():
        carry_ref[...] = jnp.zeros_like(carry_ref)
        meta_tail_ref[...] = jnp.zeros_like(meta_tail_ref)

    h = h_ref[...]
    row = lax.broadcasted_iota(jnp.int32, (TM, 1), 0)
    valid = jnp.logical_or(i > 0, row >= META_ROW0)
    hb = jnp.where(valid, h, 0.0).astype(jnp.bfloat16)
    batch_start = jnp.logical_and(i > 0, (i - 1) % tiles_per_batch == 0)
    FC = FFN_CHUNK
    for c in range(N_FFN_CHUNKS):
        cols = slice(c * 2 * FC, (c + 1) * 2 * FC)
        u = jnp.concatenate(
            [jnp.dot(hb, win_ref[:, c * FC:(c + 1) * FC], preferred_element_type=jnp.float32),
             jnp.dot(hb, win_ref[:, D_FF + c * FC:D_FF + (c + 1) * FC], preferred_element_type=jnp.float32)],
            axis=1)
        prev = jnp.where(batch_start, meta_tail_ref[:, cols], carry_ref[:, cols])
        p1 = prev[SUBLANES - 1:SUBLANES, :]
        p2 = prev[SUBLANES - 2:SUBLANES - 1, :]
        u1 = jnp.where(row == 0, p1, pltpu.roll(u, 1, axis=0))
        u2 = jnp.where(row == 0, p2, jnp.where(row == 1, p1, pltpu.roll(u, 2, axis=0)))
        cw = cw_ref[c]
        cv = cw[3:4, :] + cw[0:1, :] * u2 + cw[1:2, :] * u1 + cw[2:3, :] * u
        tail = u[TM - SUBLANES:, :]
        carry_ref[:, cols] = tail
        meta_tail_ref[:, cols] = jnp.where(i == 0, tail, meta_tail_ref[:, cols])

        a = cv[:, :FC]
        gate = cv[:, FC:]
        act_ref[:, c * FC:(c + 1) * FC] = (a * jax.nn.sigmoid(a) * gate).astype(jnp.bfloat16)
    ffn = jnp.dot(act_ref[...], wout_ref[...], preferred_element_type=jnp.float32)
    o_ref[...] = _layer_norm(ALPHA * h + ffn, g_ref[...], b_ref[...])


def _ffn_ln(h, win, cw_r, wout, g, b, tiles_per_batch, drop_meta):
    rows = h.shape[0]
    n_tiles = rows // TM
    if drop_meta:
        out_rows, out_map = rows - TM, (lambda i: (jnp.maximum(i - 1, 0), 0))
    else:
        out_rows, out_map = rows, (lambda i: (i, 0))
    return pl.pallas_call(
        functools.partial(_ffn_kernel, tiles_per_batch),
        grid=(n_tiles,),
        in_specs=[
            pl.BlockSpec((TM, D_MODEL), lambda i: (i, 0)),
            _const_spec(win.shape),
            _const_spec(cw_r.shape),
            _const_spec(wout.shape),
            _const_spec(g.shape),
            _const_spec(b.shape),
        ],
        out_specs=pl.BlockSpec((TM, D_MODEL), out_map),
        out_shape=jax.ShapeDtypeStruct((out_rows, D_MODEL), jnp.float32),
        scratch_shapes=[
            pltpu.VMEM((TM, D_FF), jnp.bfloat16),
            pltpu.VMEM((SUBLANES, 2 * D_FF), jnp.float32),
            pltpu.VMEM((SUBLANES, 2 * D_FF), jnp.float32),
        ],
        compiler_params=_params(1),
        name="ffn_ln",
    )(h, win, cw_r, wout, g, b)


def _qkv_kernel(tiles_per_batch, h_ref, wq_ref, wkv_ref, wf_ref, bf_ref,
                q_ref, k_ref, v_ref, qa_ref, ka_ref, carry_ref, meta_c_ref):
    i = pl.program_id(0)

    @pl.when(i == 0)
    def _():
        carry_ref[...] = jnp.zeros_like(carry_ref)

    hb = h_ref[...].astype(jnp.bfloat16)
    z = jnp.dot(hb, wf_ref[...], preferred_element_type=jnp.float32) + bf_ref[...]
    q = jnp.dot(hb, wq_ref[...], preferred_element_type=jnp.float32)
    q_ref[...] = (q * (HEAD_DIM ** -0.5 * LOG2E)).astype(jnp.bfloat16)
    k_ref[...] = jnp.dot(hb, wkv_ref[:, :D_MODEL], preferred_element_type=jnp.float32).astype(jnp.bfloat16)
    v_ref[...] = jnp.dot(hb, wkv_ref[:, D_MODEL:], preferred_element_type=jnp.float32).astype(jnp.bfloat16)

    logf = jnp.minimum(z, 0.0) - jnp.log1p(jnp.exp(-jnp.abs(z)))
    row = lax.broadcasted_iota(jnp.int32, (TM, 1), 0)
    valid = jnp.logical_or(i > 0, row >= META_ROW0)
    x = jnp.where(valid, logf, 0.0)
    s = 1
    while s < TM:
        x = x + jnp.where(row >= s, pltpu.roll(x, s, axis=0), 0.0)
        s *= 2
    batch_start = jnp.logical_and(i > 0, (i - 1) % tiles_per_batch == 0)
    last = SUBLANES - 1
    base = jnp.where(batch_start, meta_c_ref[last:, :], carry_ref[last:, :])
    c = x + base
    tail = c[TM - SUBLANES:, :]
    carry_ref[...] = tail

    @pl.when(i == 0)
    def _():
        meta_c_ref[...] = tail

    c2 = c * LOG2E
    hi = c2.astype(jnp.bfloat16)
    r1 = c2 - hi.astype(jnp.float32)
    mid = r1.astype(jnp.bfloat16)
    lo = (r1 - mid.astype(jnp.float32)).astype(jnp.bfloat16)
    src = lax.broadcasted_iota(jnp.int32, (LANES, LANES), 0)
    dst = lax.broadcasted_iota(jnp.int32, (LANES, LANES), 1)

    def spread(piece, offset):
        sel = jnp.logical_and(dst == AUG * src + offset, src < N_HEADS).astype(jnp.bfloat16)
        return jnp.dot(piece, sel, preferred_element_type=jnp.float32)

    slot = lax.broadcasted_iota(jnp.int32, (1, LANES), 1) % AUG
    ones_q = jnp.logical_and(slot >= 3, slot < 6).astype(jnp.float32)
    ones_k = (slot < 3).astype(jnp.float32)
    qa_ref[...] = (spread(hi, 0) + spread(mid, 1) + spread(lo, 2) + ones_q).astype(jnp.bfloat16)
    ka_ref[...] = (ones_k - (spread(hi, 3) + spread(mid, 4) + spread(lo, 5))).astype(jnp.bfloat16)


def _qkv(h, wq, wkv, wf_pad, bf_pad, tiles_per_batch):
    rows = h.shape[0]
    n_tiles = rows // TM
    tile = pl.BlockSpec((TM, D_MODEL), lambda i: (i, 0))
    aug_tile = pl.BlockSpec((TM, LANES), lambda i: (i, 0))
    kv_pos = lambda i: (jnp.where(i == 0, n_tiles - 1, i - 1), 0)
    kv_tile = pl.BlockSpec((TM, D_MODEL), kv_pos)
    kv_aug_tile = pl.BlockSpec((TM, LANES), kv_pos)
    wide = jax.ShapeDtypeStruct((rows, D_MODEL), jnp.bfloat16)
    narrow = jax.ShapeDtypeStruct((rows, LANES), jnp.bfloat16)
    return pl.pallas_call(
        functools.partial(_qkv_kernel, tiles_per_batch),
        grid=(n_tiles,),
        in_specs=[tile, _const_spec(wq.shape), _const_spec(wkv.shape),
                  _const_spec(wf_pad.shape), _const_spec(bf_pad.shape)],
        out_specs=[tile, kv_tile, kv_tile, aug_tile, kv_aug_tile],
        out_shape=[wide, wide, wide, narrow, narrow],
        scratch_shapes=[pltpu.VMEM((SUBLANES, LANES), jnp.float32),
                        pltpu.VMEM((SUBLANES, LANES), jnp.float32)],
        compiler_params=_params(1),
        name="qkv_proj",
    )(h, wq, wkv, wf_pad, bf_pad)


KV_GROUP = 2
MODE_META, MODE_FULL, MODE_TAIL1, MODE_TAIL2 = 0, 1, 2, 3


def _attention_steps(n_tiles, tiles_per_batch):
    f_tab, kv_tab, first, last, mode, diag = [], [], [], [], [], []
    meta_block = (n_tiles - 1) // KV_GROUP
    for f in range(n_tiles):
        steps = [(meta_block, MODE_META)]
        if f > 0:
            b, qi = divmod(f - 1, tiles_per_batch)
            base = b * tiles_per_batch // KV_GROUP
            steps += [(base + j, MODE_FULL) for j in range(qi // 2)]
            steps.append((base + qi // 2, MODE_TAIL1 if qi % 2 == 0 else MODE_TAIL2))
        for n, (kv, md) in enumerate(steps):
            f_tab.append(f)
            kv_tab.append(kv)
            first.append(int(n == 0))
            last.append(int(n == len(steps) - 1))
            mode.append(md)
            diag.append(int(f == 0))
    return [np.asarray(t, np.int32) for t in (f_tab, kv_tab, first, last, mode, diag)]


def _attn_kernel(n_tiles, f_tab, kv_tab, first_tab, last_tab, mode_tab, diag_tab,
                 q_ref, qa_ref, k_ref, ka_ref, v_ref, o_ref, m_ref, acc_ref):
    del f_tab, kv_tab, first_tab
    step_id = pl.program_id(0)
    mode = mode_tab[step_id]

    lane = lax.broadcasted_iota(jnp.int32, (1, LANES), 1)
    low_half = lane < HEAD_DIM
    nt_dims = (((1,), (1,)), ((), ()))

    def step(subs):
        n_pairs = N_HEADS // 2
        qa = qa_ref[...]
        k0_of = {"meta": TM - LANES, "full": 0, "diag": 0}
        if any(kind == "meta" for _, kind in subs):
            qrow = lax.broadcasted_iota(jnp.int32, (TM, LANES), 0)
            krow = lax.broadcasted_iota(jnp.int32, (TM, LANES), 1) + (TM - LANES)
            causal = diag_tab[step_id] == 1
            keep = jnp.logical_and(krow >= META_ROW0, jnp.logical_or(krow <= qrow, jnp.logical_not(causal)))

        def pair_scores(item):
            sub, kind, p = item
            r0, r1 = sub * TM + k0_of[kind], (sub + 1) * TM
            cols = slice(p * LANES, (p + 1) * LANES)
            qp = q_ref[:, cols]
            k_ext = jnp.concatenate([k_ref[r0:r1, cols], ka_ref[r0:r1, :]], axis=1)
            vp = v_ref[r0:r1, cols]
            scores, vms = [], []
            for hh in range(2):
                hd = 2 * p + hh
                in_head = low_half if hh == 0 else jnp.logical_not(low_half)
                ones_lane = HEAD_DIM if hh == 0 else 0
                aug_lanes = jnp.logical_and(lane >= AUG * hd, lane < AUG * (hd + 1))
                q_ext = jnp.concatenate([jnp.where(in_head, qp, jnp.zeros_like(qp)),
                                         jnp.where(aug_lanes, qa, jnp.zeros_like(qa))], axis=1)
                vms.append(jnp.where(in_head, vp, jnp.where(lane == ones_lane, 1.0, 0.0).astype(vp.dtype)))
                scores.append(lax.dot_general(q_ext, k_ext, nt_dims, preferred_element_type=jnp.float32))
            return scores, vms

        def softmax_pv(kind, hd, s, vm):
            n_kblk = s.shape[1] // LANES
            if kind == "meta":
                s = jnp.where(keep, s, NEG_INF)
                m_new = jnp.broadcast_to(jnp.max(s, axis=1, keepdims=True), (TM, LANES))
                pe = jnp.exp2(s - m_new).astype(jnp.bfloat16)
                acc_ref[hd] = jnp.dot(pe, vm, preferred_element_type=jnp.float32)
                m_ref[hd] = m_new
                return
            m_prev = m_ref[hd]
            if kind == "diag":
                tri = (lax.broadcasted_iota(jnp.int32, (LANES, LANES), 1)
                       <= lax.broadcasted_iota(jnp.int32, (LANES, LANES), 0))
                blk = lambda qb, kb: s[qb * LANES:(qb + 1) * LANES, kb * LANES:(kb + 1) * LANES]
                diag_blk = [jnp.where(tri, blk(b, b), NEG_INF) for b in range(n_kblk)]
                m_rows = []
                for qb in range(n_kblk):
                    m_part = diag_blk[qb]
                    for kb in range(qb):
                        m_part = jnp.maximum(m_part, blk(qb, kb))
                    m_rows.append(jnp.max(m_part, axis=1, keepdims=True))
                m_new = jnp.maximum(m_prev, jnp.concatenate(m_rows, axis=0))
                p_rows = []
                for qb in range(n_kblk):
                    mq = m_new[qb * LANES:(qb + 1) * LANES, :]
                    p_rows.append(jnp.concatenate(
                        [jnp.exp2(blk(qb, kb) - mq) for kb in range(qb)]
                        + [jnp.exp2(diag_blk[qb] - mq)]
                        + [jnp.zeros((LANES, LANES), jnp.float32)] * (n_kblk - 1 - qb), axis=1))
                pe = jnp.concatenate(p_rows, axis=0).astype(jnp.bfloat16)
            else:
                m_part = s[:, :LANES]
                for kb in range(1, n_kblk):
                    m_part = jnp.maximum(m_part, s[:, kb * LANES:(kb + 1) * LANES])
                m_new = jnp.maximum(m_prev, jnp.max(m_part, axis=1, keepdims=True))
                pe = jnp.concatenate([jnp.exp2(s[:, kb * LANES:(kb + 1) * LANES] - m_new)
                                      for kb in range(n_kblk)], axis=1).astype(jnp.bfloat16)
            alpha = jnp.exp2(m_prev - m_new)
            pv = jnp.dot(pe, vm, preferred_element_type=jnp.float32)
            acc_ref[hd] = alpha * acc_ref[hd] + pv
            m_ref[hd] = m_new

        work = [(sub, kind, p) for sub, kind in subs for p in range(n_pairs)]
        cur = pair_scores(work[0])
        for n, (sub, kind, p) in enumerate(work):
            nxt = pair_scores(work[n + 1]) if n + 1 < len(work) else None
            for hh in range(2):
                softmax_pv(kind, 2 * p + hh, cur[0][hh], cur[1][hh])
            cur = nxt

    @pl.when(mode == MODE_META)
    def _():
        step([((n_tiles - 1) % KV_GROUP, "meta")])

    @pl.when(mode == MODE_FULL)
    def _():
        step([(0, "full"), (1, "full")])

    @pl.when(mode == MODE_TAIL1)
    def _():
        step([(0, "diag")])

    @pl.when(mode == MODE_TAIL2)
    def _():
        step([(1, "diag"), (0, "full")])

    @pl.when(last_tab[step_id] == 1)
    def _():
        for p in range(N_HEADS // 2):
            cols = slice(p * LANES, (p + 1) * LANES)
            a0 = acc_ref[2 * p]
            a1 = acc_ref[2 * p + 1]
            l0 = a0[:, HEAD_DIM:HEAD_DIM + 1]
            l1 = a1[:, 0:1]
            o_ref[:, cols] = jnp.where(low_half, a0 / l0, a1 / l1).astype(o_ref.dtype)


def _attention(q, qa, k, ka, v, tiles_per_batch):
    rows = q.shape[0]
    assert tiles_per_batch % KV_GROUP == 0
    tabs = _attention_steps(rows // TM, tiles_per_batch)
    q_map = lambda s, f_tab, kv_tab, *_: (f_tab[s], 0)
    kv_map = lambda s, f_tab, kv_tab, *_: (kv_tab[s], 0)
    grid_spec = pltpu.PrefetchScalarGridSpec(
        num_scalar_prefetch=len(tabs),
        grid=(len(tabs[0]),),
        in_specs=[
            pl.BlockSpec((TM, D_MODEL), q_map),
            pl.BlockSpec((TM, LANES), q_map),
            pl.BlockSpec((KV_GROUP * TM, D_MODEL), kv_map),
            pl.BlockSpec((KV_GROUP * TM, LANES), kv_map),
            pl.BlockSpec((KV_GROUP * TM, D_MODEL), kv_map),
        ],
        out_specs=pl.BlockSpec((TM, D_MODEL), q_map),
        scratch_shapes=[
            pltpu.VMEM((N_HEADS, TM, LANES), jnp.float32),
            pltpu.VMEM((N_HEADS, TM, LANES), jnp.float32),
        ],
    )
    return pl.pallas_call(
        functools.partial(_attn_kernel, rows // TM),
        grid_spec=grid_spec,
        out_shape=jax.ShapeDtypeStruct((rows, D_MODEL), jnp.bfloat16),
        compiler_params=_params(1),
        name="fox_attention",
    )(*[jnp.asarray(t) for t in tabs], q, qa, k, ka, v)


def _oproj_ln_kernel(h_ref, o_ref, wo_ref, g_ref, b_ref, out_ref):
    mix = jnp.dot(o_ref[...], wo_ref[...], preferred_element_type=jnp.float32)
    out_ref[...] = _layer_norm(ALPHA * h_ref[...] + mix, g_ref[...], b_ref[...])


def _oproj_ln(h, o, wo, g, b):
    rows = h.shape[0]
    tile = pl.BlockSpec((TM, D_MODEL), lambda i: (i, 0))
    return pl.pallas_call(
        _oproj_ln_kernel,
        grid=(rows // TM,),
        in_specs=[tile, tile, _const_spec(wo.shape), _const_spec(g.shape), _const_spec(b.shape)],
        out_specs=tile,
        out_shape=jax.ShapeDtypeStruct((rows, D_MODEL), jnp.float32),
        compiler_params=_params(1),
        name="oproj_ln",
    )(h, o, wo, g, b)


def _ffn_weights(w_in, conv_w, conv_b, w_out):
    F, FC, NC = D_FF, FFN_CHUNK, N_FFN_CHUNKS

    def regroup(m):
        a = m[..., :F].reshape(m.shape[:-1] + (NC, FC))
        g = m[..., F:].reshape(m.shape[:-1] + (NC, FC))
        return jnp.moveaxis(jnp.concatenate([a, g], axis=-1), -2, 0)

    win = w_in.astype(jnp.bfloat16)
    taps = jnp.concatenate([conv_w, conv_b[None, :],
                            jnp.zeros((SUBLANES - CONV_WIDTH - 1, 2 * F), conv_w.dtype)], axis=0)
    cw_r = regroup(taps)
    return win, cw_r, w_out.astype(jnp.bfloat16)


def kernel(x, meta, pool_w, pool_scale, w_kv, w_f, b_f, w_q, w_o, ffn_w_in, ffn_conv_w, ffn_conv_b,
           ffn_w_out, ln_g, ln_b):
    B, S, D = x.shape
    assert D == D_MODEL and S % TM == 0 and meta.shape == (N_META, D_MODEL)
    assert pool_w.shape[0] == N_A_LAYERS == 1 and w_q.shape[0] == DEPTH - N_A_LAYERS == 1
    tpb = S // TM

    meta_tile = jnp.concatenate([jnp.zeros((META_ROW0, D), x.dtype), meta.astype(x.dtype)], axis=0)
    row2 = lambda a: a.reshape(1, -1)

    h = _pool_ln(x.reshape(B * S, D), meta_tile, pool_w[0].astype(jnp.bfloat16), row2(pool_scale[0]),
                 row2(ln_g[0, 0]), row2(ln_b[0, 0]), tpb)
    h = _ffn_ln(h, *_ffn_weights(ffn_w_in[0], ffn_conv_w[0], ffn_conv_b[0], ffn_w_out[0]),
                row2(ln_g[0, 1]), row2(ln_b[0, 1]), tpb, drop_meta=False)

    wf_pad = jnp.pad(w_f, ((0, 0), (0, LANES - N_HEADS))).astype(jnp.bfloat16)
    bf_pad = jnp.pad(b_f, (0, LANES - N_HEADS)).reshape(1, LANES)
    q, k, v, qa, ka = _qkv(h, w_q[0].astype(jnp.bfloat16), w_kv.astype(jnp.bfloat16), wf_pad, bf_pad, tpb)
    o = _attention(q, qa, k, ka, v, tpb)
    h = _oproj_ln(h, o, w_o[0].astype(jnp.bfloat16), row2(ln_g[1, 0]), row2(ln_b[1, 0]))
    h = _ffn_ln(h, *_ffn_weights(ffn_w_in[1], ffn_conv_w[1], ffn_conv_b[1], ffn_w_out[1]),
                row2(ln_g[1, 1]), row2(ln_b[1, 1]), tpb, drop_meta=True)
    return h.reshape(B, S, D)
```

```python
import functools
import math

import numpy as np
import jax
import jax.numpy as jnp
from jax import lax
from jax.experimental import pallas as pl
from jax.experimental.pallas import tpu as pltpu

D_MODEL = 1024
DEPTH = 2
N_META = 16
POOL_WINDOWS = (2, 4, 8, 16)
N_POOL_GROUPS = len(POOL_WINDOWS)
POOL_GROUP_DIM = D_MODEL // N_POOL_GROUPS
HEAD_DIM = 64
N_HEADS = D_MODEL // HEAD_DIM
D_FF = ((8 * D_MODEL // 3 + 127) // 128) * 128
CONV_WIDTH = 3
N_A_LAYERS = DEPTH // 2
ALPHA = (2.0 * DEPTH) ** 0.25
LN_EPS = 1e-5
NEG_INF = -1e30
LOG2E = math.log2(math.e)

TM = 512
META_ROW0 = TM - N_META
HALO = 16
FFN_CHUNK = 256
N_FFN_CHUNKS = D_FF // FFN_CHUNK
LANES = 128
SUBLANES = 8
AUG = LANES // N_HEADS
VMEM_LIMIT_BYTES = 56 * 1024 * 1024

assert D_FF % FFN_CHUNK == 0 and AUG >= 6


def _layer_norm(y, g, b):
    mu = jnp.mean(y, axis=-1, keepdims=True)
    xc = y - mu
    var = jnp.mean(xc * xc, axis=-1, keepdims=True)
    return xc * lax.rsqrt(var + LN_EPS) * g + b


def _const_spec(shape):
    nd = len(shape)
    return pl.BlockSpec(shape, lambda *_: (0,) * nd, pipeline_mode=pl.Buffered(1))


def _params(n_axes):
    return pltpu.CompilerParams(dimension_semantics=("arbitrary",) * n_axes,
                                vmem_limit_bytes=VMEM_LIMIT_BYTES)


def _pool_ln_kernel(tiles_per_batch, x_ref, xhalo_ref, meta_ref, w_ref, scale_ref, g_ref, b_ref, o_ref):
    i = pl.program_id(0)
    is_meta = i == 0
    meta_tile = meta_ref[...]
    h = jnp.where(is_meta, meta_tile, x_ref[...])
    batch_start = (i - 1) % tiles_per_batch == 0
    halo = jnp.where(batch_start, meta_tile[TM - HALO:, :], xhalo_ref[...])
    halo = jnp.where(is_meta, 0.0, halo)
    row = lax.broadcasted_iota(jnp.int32, (TM, 1), 0)
    t = jnp.where(is_meta, row - (META_ROW0 - 1), HALO)
    outs = []
    G = POOL_GROUP_DIM
    for gi, w in enumerate(POOL_WINDOWS):
        sl = slice(gi * G, (gi + 1) * G)
        hg = h[:, sl]
        s = jnp.concatenate([halo[:, sl], hg], axis=0)
        k = 1
        while k < w:
            s = s + pltpu.roll(s, k, axis=0)
            k *= 2
        count = jnp.clip(t, 1, w).astype(jnp.float32)
        pooled = s[HALO:, :] / count
        diff = (pooled - hg).astype(jnp.bfloat16)
        mixed = jnp.dot(diff, w_ref[gi], preferred_element_type=jnp.float32)
        outs.append(ALPHA * hg + mixed * scale_ref[:, sl])
    y = jnp.concatenate(outs, axis=1)
    o_ref[...] = _layer_norm(y, g_ref[...], b_ref[...])


def _pool_ln(x_flat, meta_tile, w_bf16, scale, g, b, tiles_per_batch):
    n_tiles = x_flat.shape[0] // TM + 1
    rows = n_tiles * TM
    halo_per_tile = TM // HALO

    def halo_map(i):
        blk = jnp.maximum((i - 1) * halo_per_tile - 1, 0)
        return (jnp.where((i - 1) % tiles_per_batch == 0, 0, blk), 0)

    return pl.pallas_call(
        functools.partial(_pool_ln_kernel, tiles_per_batch),
        grid=(n_tiles,),
        in_specs=[
            pl.BlockSpec((TM, D_MODEL), lambda i: (jnp.maximum(i - 1, 0), 0)),
            pl.BlockSpec((HALO, D_MODEL), halo_map),
            _const_spec(meta_tile.shape),
            _const_spec(w_bf16.shape),
            _const_spec(scale.shape),
            _const_spec(g.shape),
            _const_spec(b.shape),
        ],
        out_specs=pl.BlockSpec((TM, D_MODEL), lambda i: (i, 0)),
        out_shape=jax.ShapeDtypeStruct((rows, D_MODEL), jnp.float32),
        compiler_params=_params(1),
        name="pool_ln",
    )(x_flat, x_flat, meta_tile, w_bf16, scale, g, b)


def _ffn_kernel(tiles_per_batch, h_ref, win_ref, cw_ref, wout_ref, g_ref, b_ref, o_ref,
                act_ref, carry_ref, meta_tail_ref):
    i = pl.program_id(0)

    @pl.when(i == 0)
    def _():
        carry_ref[...] = jnp.zeros_like(carry_ref)
        meta_tail_ref[...] = jnp.zeros_like(meta_tail_ref)

    h = h_ref[...]
    row = lax.broadcasted_iota(jnp.int32, (TM, 1), 0)
    valid = jnp.logical_or(i > 0, row >= META_ROW0)
    hb = jnp.where(valid, h, 0.0).astype(jnp.bfloat16)
    batch_start = jnp.logical_and(i > 0, (i - 1) % tiles_per_batch == 0)
    FC = FFN_CHUNK
    for c in range(N_FFN_CHUNKS):
        cols = slice(c * 2 * FC, (c + 1) * 2 * FC)
        u = jnp.concatenate(
            [jnp.dot(hb, win_ref[:, c * FC:(c + 1) * FC], preferred_element_type=jnp.float32),
             jnp.dot(hb, win_ref[:, D_FF + c * FC:D_FF + (c + 1) * FC], preferred_element_type=jnp.float32)],
            axis=1)
        prev = jnp.where(batch_start, meta_tail_ref[:, cols], carry_ref[:, cols])
        p1 = prev[SUBLANES - 1:SUBLANES, :]
        p2 = prev[SUBLANES - 2:SUBLANES - 1, :]
        u1 = jnp.where(row == 0, p1, pltpu.roll(u, 1, axis=0))
        u2 = jnp.where(row == 0, p2, jnp.where(row == 1, p1, pltpu.roll(u, 2, axis=0)))
        cw = cw_ref[c]
        cv = cw[3:4, :] + cw[0:1, :] * u2 + cw[1:2, :] * u1 + cw[2:3, :] * u
        tail = u[TM - SUBLANES:, :]
        carry_ref[:, cols] = tail
        meta_tail_ref[:, cols] = jnp.where(i == 0, tail, meta_tail_ref[:, cols])

        a = cv[:, :FC]
        gate = cv[:, FC:]
        act_ref[:, c * FC:(c + 1) * FC] = (a * jax.nn.sigmoid(a) * gate).astype(jnp.bfloat16)
    ffn = jnp.dot(act_ref[...], wout_ref[...], preferred_element_type=jnp.float32)
    o_ref[...] = _layer_norm(ALPHA * h + ffn, g_ref[...], b_ref[...])


def _ffn_ln(h, win, cw_r, wout, g, b, tiles_per_batch, drop_meta):
    rows = h.shape[0]
    n_tiles = rows // TM
    if drop_meta:
        out_rows, out_map = rows - TM, (lambda i: (jnp.maximum(i - 1, 0), 0))
    else:
        out_rows, out_map = rows, (lambda i: (i, 0))
    return pl.pallas_call(
        functools.partial(_ffn_kernel, tiles_per_batch),
        grid=(n_tiles,),
        in_specs=[
            pl.BlockSpec((TM, D_MODEL), lambda i: (i, 0)),
            _const_spec(win.shape),
            _const_spec(cw_r.shape),
            _const_spec(wout.shape),
            _const_spec(g.shape),
            _const_spec(b.shape),
        ],
        out_specs=pl.BlockSpec((TM, D_MODEL), out_map),
        out_shape=jax.ShapeDtypeStruct((out_rows, D_MODEL), jnp.float32),
        scratch_shapes=[
            pltpu.VMEM((TM, D_FF), jnp.bfloat16),
            pltpu.VMEM((SUBLANES, 2 * D_FF), jnp.float32),
            pltpu.VMEM((SUBLANES, 2 * D_FF), jnp.float32),
        ],
        compiler_params=_params(1),
        name="ffn_ln",
    )(h, win, cw_r, wout, g, b)


def _qkv_kernel(tiles_per_batch, h_ref, wq_ref, wkv_ref, wf_ref, bf_ref,
                q_ref, k_ref, v_ref, qa_ref, ka_ref, carry_ref, meta_c_ref):
    i = pl.program_id(0)

    @pl.when(i == 0)
    def _():
        carry_ref[...] = jnp.zeros_like(carry_ref)

    hb = h_ref[...].astype(jnp.bfloat16)
    z = jnp.dot(hb, wf_ref[...], preferred_element_type=jnp.float32) + bf_ref[...]
    q = jnp.dot(hb, wq_ref[...], preferred_element_type=jnp.float32)
    q_ref[...] = (q * (HEAD_DIM ** -0.5 * LOG2E)).astype(jnp.bfloat16)
    k_ref[...] = jnp.dot(hb, wkv_ref[:, :D_MODEL], preferred_element_type=jnp.float32).astype(jnp.bfloat16)
    v_ref[...] = jnp.dot(hb, wkv_ref[:, D_MODEL:], preferred_element_type=jnp.float32).astype(jnp.bfloat16)

    logf = jnp.minimum(z, 0.0) - jnp.log1p(jnp.exp(-jnp.abs(z)))
    row = lax.broadcasted_iota(jnp.int32, (TM, 1), 0)
    valid = jnp.logical_or(i > 0, row >= META_ROW0)
    x = jnp.where(valid, logf, 0.0)
    s = 1
    while s < TM:
        x = x + jnp.where(row >= s, pltpu.roll(x, s, axis=0), 0.0)
        s *= 2
    batch_start = jnp.logical_and(i > 0, (i - 1) % tiles_per_batch == 0)
    last = SUBLANES - 1
    base = jnp.where(batch_start, meta_c_ref[last:, :], carry_ref[last:, :])
    c = x + base
    tail = c[TM - SUBLANES:, :]
    carry_ref[...] = tail

    @pl.when(i == 0)
    def _():
        meta_c_ref[...] = tail

    c2 = c * LOG2E
    hi = c2.astype(jnp.bfloat16)
    r1 = c2 - hi.astype(jnp.float32)
    mid = r1.astype(jnp.bfloat16)
    lo = (r1 - mid.astype(jnp.float32)).astype(jnp.bfloat16)
    src = lax.broadcasted_iota(jnp.int32, (LANES, LANES), 0)
    dst = lax.broadcasted_iota(jnp.int32, (LANES, LANES), 1)

    def spread(piece, offset):
        sel = jnp.logical_and(dst == AUG * src + offset, src < N_HEADS).astype(jnp.bfloat16)
        return jnp.dot(piece, sel, preferred_element_type=jnp.float32)

    slot = lax.broadcasted_iota(jnp.int32, (1, LANES), 1) % AUG
    ones_q = jnp.logical_and(slot >= 3, slot < 6).astype(jnp.float32)
    ones_k = (slot < 3).astype(jnp.float32)
    qa_ref[...] = (spread(hi, 0) + spread(mid, 1) + spread(lo, 2) + ones_q).astype(jnp.bfloat16)
    ka_ref[...] = (ones_k - (spread(hi, 3) + spread(mid, 4) + spread(lo, 5))).astype(jnp.bfloat16)


def _qkv(h, wq, wkv, wf_pad, bf_pad, tiles_per_batch):
    rows = h.shape[0]
    n_tiles = rows // TM
    tile = pl.BlockSpec((TM, D_MODEL), lambda i: (i, 0))
    aug_tile = pl.BlockSpec((TM, LANES), lambda i: (i, 0))
    kv_pos = lambda i: (jnp.where(i == 0, n_tiles - 1, i - 1), 0)
    kv_tile = pl.BlockSpec((TM, D_MODEL), kv_pos)
    kv_aug_tile = pl.BlockSpec((TM, LANES), kv_pos)
    wide = jax.ShapeDtypeStruct((rows, D_MODEL), jnp.bfloat16)
    narrow = jax.ShapeDtypeStruct((rows, LANES), jnp.bfloat16)
    return pl.pallas_call(
        functools.partial(_qkv_kernel, tiles_per_batch),
        grid=(n_tiles,),
        in_specs=[tile, _const_spec(wq.shape), _const_spec(wkv.shape),
                  _const_spec(wf_pad.shape), _const_spec(bf_pad.shape)],
        out_specs=[tile, kv_tile, kv_tile, aug_tile, kv_aug_tile],
        out_shape=[wide, wide, wide, narrow, narrow],
        scratch_shapes=[pltpu.VMEM((SUBLANES, LANES), jnp.float32),
                        pltpu.VMEM((SUBLANES, LANES), jnp.float32)],
        compiler_params=_params(1),
        name="qkv_proj",
    )(h, wq, wkv, wf_pad, bf_pad)


KV_GROUP = 2
MODE_META, MODE_FULL, MODE_TAIL1, MODE_TAIL2 = 0, 1, 2, 3


def _attention_steps(n_tiles, tiles_per_batch):
    f_tab, kv_tab, first, last, mode, diag = [], [], [], [], [], []
    meta_block = (n_tiles - 1) // KV_GROUP
    for f in range(n_tiles):
        steps = [(meta_block, MODE_META)]
        if f > 0:
            b, qi = divmod(f - 1, tiles_per_batch)
            base = b * tiles_per_batch // KV_GROUP
            steps += [(base + j, MODE_FULL) for j in range(qi // 2)]
            steps.append((base + qi // 2, MODE_TAIL1 if qi % 2 == 0 else MODE_TAIL2))
        for n, (kv, md) in enumerate(steps):
            f_tab.append(f)
            kv_tab.append(kv)
            first.append(int(n == 0))
            last.append(int(n == len(steps) - 1))
            mode.append(md)
            diag.append(int(f == 0))
    return [np.asarray(t, np.int32) for t in (f_tab, kv_tab, first, last, mode, diag)]


def _attn_kernel(n_tiles, f_tab, kv_tab, first_tab, last_tab, mode_tab, diag_tab,
                 q_ref, qa_ref, k_ref, ka_ref, v_ref, o_ref, m_ref, acc_ref):
    del f_tab, kv_tab, first_tab
    step_id = pl.program_id(0)
    mode = mode_tab[step_id]

    lane = lax.broadcasted_iota(jnp.int32, (1, LANES), 1)
    low_half = lane < HEAD_DIM
    nt_dims = (((1,), (1,)), ((), ()))

    def step(subs):
        n_pairs = N_HEADS // 2
        qa = qa_ref[...]
        k0_of = {"meta": TM - LANES, "full": 0, "diag": 0}
        if any(kind == "meta" for _, kind in subs):
            qrow = lax.broadcasted_iota(jnp.int32, (TM, LANES), 0)
            krow = lax.broadcasted_iota(jnp.int32, (TM, LANES), 1) + (TM - LANES)
            causal = diag_tab[step_id] == 1
            keep = jnp.logical_and(krow >= META_ROW0, jnp.logical_or(krow <= qrow, jnp.logical_not(causal)))

        def pair_scores(item):
            sub, kind, p = item
            r0, r1 = sub * TM + k0_of[kind], (sub + 1) * TM
            cols = slice(p * LANES, (p + 1) * LANES)
            qp = q_ref[:, cols]
            k_ext = jnp.concatenate([k_ref[r0:r1, cols], ka_ref[r0:r1, :]], axis=1)
            vp = v_ref[r0:r1, cols]
            scores, vms = [], []
            for hh in range(2):
                hd = 2 * p + hh
                in_head = low_half if hh == 0 else jnp.logical_not(low_half)
                ones_lane = HEAD_DIM if hh == 0 else 0
                aug_lanes = jnp.logical_and(lane >= AUG * hd, lane < AUG * (hd + 1))
                q_ext = jnp.concatenate([jnp.where(in_head, qp, jnp.zeros_like(qp)),
                                         jnp.where(aug_lanes, qa, jnp.zeros_like(qa))], axis=1)
                vms.append(jnp.where(in_head, vp, jnp.where(lane == ones_lane, 1.0, 0.0).astype(vp.dtype)))
                scores.append(lax.dot_general(q_ext, k_ext, nt_dims, preferred_element_type=jnp.float32))
            return scores, vms

        def softmax_pv(kind, hd, s, vm):
            n_kblk = s.shape[1] // LANES
            if kind == "meta":
                s = jnp.where(keep, s, NEG_INF)
                m_new = jnp.broadcast_to(jnp.max(s, axis=1, keepdims=True), (TM, LANES))
                pe = jnp.exp2(s - m_new).astype(jnp.bfloat16)
                acc_ref[hd] = jnp.dot(pe, vm, preferred_element_type=jnp.float32)
                m_ref[hd] = m_new
                return
            m_prev = m_ref[hd]
            if kind == "diag":
                tri = (lax.broadcasted_iota(jnp.int32, (LANES, LANES), 1)
                       <= lax.broadcasted_iota(jnp.int32, (LANES, LANES), 0))
                blk = lambda qb, kb: s[qb * LANES:(qb + 1) * LANES, kb * LANES:(kb + 1) * LANES]
                diag_blk = [jnp.where(tri, blk(b, b), NEG_INF) for b in range(n_kblk)]
                m_rows = []
                for qb in range(n_kblk):
                    m_part = diag_blk[qb]
                    for kb in range(qb):
                        m_part = jnp.maximum(m_part, blk(qb, kb))
                    m_rows.append(jnp.max(m_part, axis=1, keepdims=True))
                m_new = jnp.maximum(m_prev, jnp.concatenate(m_rows, axis=0))
                p_rows = []
                for qb in range(n_kblk):
                    mq = m_new[qb * LANES:(qb + 1) * LANES, :]
                    p_rows.append(jnp.concatenate(
                        [jnp.exp2(blk(qb, kb) - mq) for kb in range(qb)]
                        + [jnp.exp2(diag_blk[qb] - mq)]
                        + [jnp.zeros((LANES, LANES), jnp.float32)] * (n_kblk - 1 - qb), axis=1))
                pe = jnp.concatenate(p_rows, axis=0).astype(jnp.bfloat16)
            else:
                m_part = s[:, :LANES]
                for kb in range(1, n_kblk):
                    m_part = jnp.maximum(m_part, s[:, kb * LANES:(kb + 1) * LANES])
                m_new = jnp.maximum(m_prev, jnp.max(m_part, axis=1, keepdims=True))
                pe = jnp.concatenate([jnp.exp2(s[:, kb * LANES:(kb + 1) * LANES] - m_new)
                                      for kb in range(n_kblk)], axis=1).astype(jnp.bfloat16)
            alpha = jnp.exp2(m_prev - m_new)
            pv = jnp.dot(pe, vm, preferred_element_type=jnp.float32)
            acc_ref[hd] = alpha * acc_ref[hd] + pv
            m_ref[hd] = m_new

        work = [(sub, kind, p) for sub, kind in subs for p in range(n_pairs)]
        cur = pair_scores(work[0])
        for n, (sub, kind, p) in enumerate(work):
            nxt = pair_scores(work[n + 1]) if n + 1 < len(work) else None
            for hh in range(2):
                softmax_pv(kind, 2 * p + hh, cur[0][hh], cur[1][hh])
            cur = nxt

    @pl.when(mode == MODE_META)
    def _():
        step([((n_tiles - 1) % KV_GROUP, "meta")])

    @pl.when(mode == MODE_FULL)
    def _():
        step([(0, "full"), (1, "full")])

    @pl.when(mode == MODE_TAIL1)
    def _():
        step([(0, "diag")])

    @pl.when(mode == MODE_TAIL2)
    def _():
        step([(1, "diag"), (0, "full")])

    @pl.when(last_tab[step_id] == 1)
    def _():
        for p in range(N_HEADS // 2):
            cols = slice(p * LANES, (p + 1) * LANES)
            a0 = acc_ref[2 * p]
            a1 = acc_ref[2 * p + 1]
            l0 = a0[:, HEAD_DIM:HEAD_DIM + 1]
            l1 = a1[:, 0:1]
            o_ref[:, cols] = jnp.where(low_half, a0 / l0, a1 / l1).astype(o_ref.dtype)


def _attention(q, qa, k, ka, v, tiles_per_batch):
    rows = q.shape[0]
    assert tiles_per_batch % KV_GROUP == 0
    tabs = _attention_steps(rows // TM, tiles_per_batch)
    q_map = lambda s, f_tab, kv_tab, *_: (f_tab[s], 0)
    kv_map = lambda s, f_tab, kv_tab, *_: (kv_tab[s], 0)
    grid_spec = pltpu.PrefetchScalarGridSpec(
        num_scalar_prefetch=len(tabs),
        grid=(len(tabs[0]),),
        in_specs=[
            pl.BlockSpec((TM, D_MODEL), q_map),
            pl.BlockSpec((TM, LANES), q_map),
            pl.BlockSpec((KV_GROUP * TM, D_MODEL), kv_map),
            pl.BlockSpec((KV_GROUP * TM, LANES), kv_map),
            pl.BlockSpec((KV_GROUP * TM, D_MODEL), kv_map),
        ],
        out_specs=pl.BlockSpec((TM, D_MODEL), q_map),
        scratch_shapes=[
            pltpu.VMEM((N_HEADS, TM, LANES), jnp.float32),
            pltpu.VMEM((N_HEADS, TM, LANES), jnp.float32),
        ],
    )
    return pl.pallas_call(
        functools.partial(_attn_kernel, rows // TM),
        grid_spec=grid_spec,
        out_shape=jax.ShapeDtypeStruct((rows, D_MODEL), jnp.bfloat16),
        compiler_params=_params(1),
        name="fox_attention",
    )(*[jnp.asarray(t) for t in tabs], q, qa, k, ka, v)


OPROJ_SUB = 2


def _oproj_ln_kernel(h_ref, o_ref, wo_ref, g_ref, b_ref, out_ref):
    for s in range(OPROJ_SUB):
        rows = slice(s * TM, (s + 1) * TM)
        mix = jnp.dot(o_ref[rows, :], wo_ref[...], preferred_element_type=jnp.float32)
        out_ref[rows, :] = _layer_norm(ALPHA * h_ref[rows, :] + mix, g_ref[...], b_ref[...])


def _oproj_ln(h, o, wo, g, b):
    rows = h.shape[0]
    tile = pl.BlockSpec((OPROJ_SUB * TM, D_MODEL), lambda i: (i, 0))
    return pl.pallas_call(
        _oproj_ln_kernel,
        grid=(pl.cdiv(rows, OPROJ_SUB * TM),),
        in_specs=[tile, tile, _const_spec(wo.shape), _const_spec(g.shape), _const_spec(b.shape)],
        out_specs=tile,
        out_shape=jax.ShapeDtypeStruct((rows, D_MODEL), jnp.float32),
        compiler_params=_params(1),
        name="oproj_ln",
    )(h, o, wo, g, b)


def _ffn_weights(w_in, conv_w, conv_b, w_out):
    F, FC, NC = D_FF, FFN_CHUNK, N_FFN_CHUNKS

    def regroup(m):
        a = m[..., :F].reshape(m.shape[:-1] + (NC, FC))
        g = m[..., F:].reshape(m.shape[:-1] + (NC, FC))
        return jnp.moveaxis(jnp.concatenate([a, g], axis=-1), -2, 0)

    win = w_in.astype(jnp.bfloat16)
    taps = jnp.concatenate([conv_w, conv_b[None, :],
                            jnp.zeros((SUBLANES - CONV_WIDTH - 1, 2 * F), conv_w.dtype)], axis=0)
    cw_r = regroup(taps)
    return win, cw_r, w_out.astype(jnp.bfloat16)


def kernel(x, meta, pool_w, pool_scale, w_kv, w_f, b_f, w_q, w_o, ffn_w_in, ffn_conv_w, ffn_conv_b,
           ffn_w_out, ln_g, ln_b):
    B, S, D = x.shape
    assert D == D_MODEL and S % TM == 0 and meta.shape == (N_META, D_MODEL)
    assert pool_w.shape[0] == N_A_LAYERS == 1 and w_q.shape[0] == DEPTH - N_A_LAYERS == 1
    tpb = S // TM

    meta_tile = jnp.concatenate([jnp.zeros((META_ROW0, D), x.dtype), meta.astype(x.dtype)], axis=0)
    row2 = lambda a: a.reshape(1, -1)

    h = _pool_ln(x.reshape(B * S, D), meta_tile, pool_w[0].astype(jnp.bfloat16), row2(pool_scale[0]),
                 row2(ln_g[0, 0]), row2(ln_b[0, 0]), tpb)
    h = _ffn_ln(h, *_ffn_weights(ffn_w_in[0], ffn_conv_w[0], ffn_conv_b[0], ffn_w_out[0]),
                row2(ln_g[0, 1]), row2(ln_b[0, 1]), tpb, drop_meta=False)

    wf_pad = jnp.pad(w_f, ((0, 0), (0, LANES - N_HEADS))).astype(jnp.bfloat16)
    bf_pad = jnp.pad(b_f, (0, LANES - N_HEADS)).reshape(1, LANES)
    q, k, v, qa, ka = _qkv(h, w_q[0].astype(jnp.bfloat16), w_kv.astype(jnp.bfloat16), wf_pad, bf_pad, tpb)
    o = _attention(q, qa, k, ka, v, tpb)
    h = _oproj_ln(h, o, w_o[0].astype(jnp.bfloat16), row2(ln_g[1, 0]), row2(ln_b[1, 0]))
    h = _ffn_ln(h, *_ffn_weights(ffn_w_in[1], ffn_conv_w[1], ffn_conv_b[1], ffn_w_out[1]),
                row2(ln_g[1, 1]), row2(ln_b[1, 1]), tpb, drop_meta=True)
    return h.reshape(B, S, D)
```

```python
import functools
import math

import numpy as np
import jax
import jax.numpy as jnp
from jax import lax
from jax.experimental import pallas as pl
from jax.experimental.pallas import tpu as pltpu

D_MODEL = 1024
DEPTH = 2
N_META = 16
POOL_WINDOWS = (2, 4, 8, 16)
N_POOL_GROUPS = len(POOL_WINDOWS)
POOL_GROUP_DIM = D_MODEL // N_POOL_GROUPS
HEAD_DIM = 64
N_HEADS = D_MODEL // HEAD_DIM
D_FF = ((8 * D_MODEL // 3 + 127) // 128) * 128
CONV_WIDTH = 3
N_A_LAYERS = DEPTH // 2
ALPHA = (2.0 * DEPTH) ** 0.25
LN_EPS = 1e-5
NEG_INF = -1e30
LOG2E = math.log2(math.e)

TM = 512
META_ROW0 = TM - N_META
HALO = 16
FFN_CHUNK = 256
N_FFN_CHUNKS = D_FF // FFN_CHUNK
LANES = 128
SUBLANES = 8
AUG = LANES // N_HEADS
VMEM_LIMIT_BYTES = 56 * 1024 * 1024

assert D_FF % FFN_CHUNK == 0 and AUG >= 6


def _layer_norm(y, g, b):
    mu = jnp.mean(y, axis=-1, keepdims=True)
    xc = y - mu
    var = jnp.mean(xc * xc, axis=-1, keepdims=True)
    return xc * lax.rsqrt(var + LN_EPS) * g + b


def _const_spec(shape):
    nd = len(shape)
    return pl.BlockSpec(shape, lambda *_: (0,) * nd, pipeline_mode=pl.Buffered(1))


def _params(n_axes):
    return pltpu.CompilerParams(dimension_semantics=("arbitrary",) * n_axes,
                                vmem_limit_bytes=VMEM_LIMIT_BYTES)


def _pool_ln_kernel(tiles_per_batch, x_ref, xhalo_ref, meta_ref, w_ref, scale_ref, g_ref, b_ref, o_ref):
    i = pl.program_id(0)
    is_meta = i == 0
    meta_tile = meta_ref[...]
    h = jnp.where(is_meta, meta_tile, x_ref[...])
    batch_start = (i - 1) % tiles_per_batch == 0
    halo = jnp.where(batch_start, meta_tile[TM - HALO:, :], xhalo_ref[...])
    halo = jnp.where(is_meta, 0.0, halo)
    row = lax.broadcasted_iota(jnp.int32, (TM, 1), 0)
    t = jnp.where(is_meta, row - (META_ROW0 - 1), HALO)
    outs = []
    G = POOL_GROUP_DIM
    for gi, w in enumerate(POOL_WINDOWS):
        sl = slice(gi * G, (gi + 1) * G)
        hg = h[:, sl]
        s = jnp.concatenate([halo[:, sl], hg], axis=0)
        k = 1
        while k < w:
            s = s + pltpu.roll(s, k, axis=0)
            k *= 2
        count = jnp.clip(t, 1, w).astype(jnp.float32)
        pooled = s[HALO:, :] / count
        diff = (pooled - hg).astype(jnp.bfloat16)
        mixed = jnp.dot(diff, w_ref[gi], preferred_element_type=jnp.float32)
        outs.append(ALPHA * hg + mixed * scale_ref[:, sl])
    y = jnp.concatenate(outs, axis=1)
    o_ref[...] = _layer_norm(y, g_ref[...], b_ref[...])


def _pool_ln(x_flat, meta_tile, w_bf16, scale, g, b, tiles_per_batch):
    n_tiles = x_flat.shape[0] // TM + 1
    rows = n_tiles * TM
    halo_per_tile = TM // HALO

    def halo_map(i):
        blk = jnp.maximum((i - 1) * halo_per_tile - 1, 0)
        return (jnp.where((i - 1) % tiles_per_batch == 0, 0, blk), 0)

    return pl.pallas_call(
        functools.partial(_pool_ln_kernel, tiles_per_batch),
        grid=(n_tiles,),
        in_specs=[
            pl.BlockSpec((TM, D_MODEL), lambda i: (jnp.maximum(i - 1, 0), 0)),
            pl.BlockSpec((HALO, D_MODEL), halo_map),
            _const_spec(meta_tile.shape),
            _const_spec(w_bf16.shape),
            _const_spec(scale.shape),
            _const_spec(g.shape),
            _const_spec(b.shape),
        ],
        out_specs=pl.BlockSpec((TM, D_MODEL), lambda i: (i, 0)),
        out_shape=jax.ShapeDtypeStruct((rows, D_MODEL), jnp.float32),
        compiler_params=_params(1),
        name="pool_ln",
    )(x_flat, x_flat, meta_tile, w_bf16, scale, g, b)


def _ffn_kernel(tiles_per_batch, h_ref, win_ref, cw_ref, wout_ref, g_ref, b_ref, o_ref,
                act_ref, carry_ref, meta_tail_ref):
    i = pl.program_id(0)

    @pl.when(i == 0)
    def _():
        carry_ref[...] = jnp.zeros_like(carry_ref)
        meta_tail_ref[...] = jnp.zeros_like(meta_tail_ref)

    h = h_ref[...]
    row = lax.broadcasted_iota(jnp.int32, (TM, 1), 0)
    valid = jnp.logical_or(i > 0, row >= META_ROW0)
    hb = jnp.where(valid, h, 0.0).astype(jnp.bfloat16)
    batch_start = jnp.logical_and(i > 0, (i - 1) % tiles_per_batch == 0)
    FC = FFN_CHUNK
    for c in range(N_FFN_CHUNKS):
        cols = slice(c * 2 * FC, (c + 1) * 2 * FC)
        u = jnp.concatenate(
            [jnp.dot(hb, win_ref[:, c * FC:(c + 1) * FC], preferred_element_type=jnp.float32),
             jnp.dot(hb, win_ref[:, D_FF + c * FC:D_FF + (c + 1) * FC], preferred_element_type=jnp.float32)],
            axis=1)
        prev = jnp.where(batch_start, meta_tail_ref[:, cols], carry_ref[:, cols])
        p1 = prev[SUBLANES - 1:SUBLANES, :]
        p2 = prev[SUBLANES - 2:SUBLANES - 1, :]
        u1 = jnp.where(row == 0, p1, pltpu.roll(u, 1, axis=0))
        u2 = jnp.where(row == 0, p2, jnp.where(row == 1, p1, pltpu.roll(u, 2, axis=0)))
        cw = cw_ref[c]
        cv = cw[3:4, :] + cw[0:1, :] * u2 + cw[1:2, :] * u1 + cw[2:3, :] * u
        tail = u[TM - SUBLANES:, :]
        carry_ref[:, cols] = tail
        meta_tail_ref[:, cols] = jnp.where(i == 0, tail, meta_tail_ref[:, cols])

        a = cv[:, :FC]
        gate = cv[:, FC:]
        act_ref[:, c * FC:(c + 1) * FC] = (a * jax.nn.sigmoid(a) * gate).astype(jnp.bfloat16)
    ffn = jnp.dot(act_ref[...], wout_ref[...], preferred_element_type=jnp.float32)
    o_ref[...] = _layer_norm(ALPHA * h + ffn, g_ref[...], b_ref[...])


def _ffn_ln(h, win, cw_r, wout, g, b, tiles_per_batch, drop_meta):
    rows = h.shape[0]
    n_tiles = rows // TM
    if drop_meta:
        out_rows, out_map = rows - TM, (lambda i: (jnp.maximum(i - 1, 0), 0))
    else:
        out_rows, out_map = rows, (lambda i: (i, 0))
    return pl.pallas_call(
        functools.partial(_ffn_kernel, tiles_per_batch),
        grid=(n_tiles,),
        in_specs=[
            pl.BlockSpec((TM, D_MODEL), lambda i: (i, 0)),
            _const_spec(win.shape),
            _const_spec(cw_r.shape),
            _const_spec(wout.shape),
            _const_spec(g.shape),
            _const_spec(b.shape),
        ],
        out_specs=pl.BlockSpec((TM, D_MODEL), out_map),
        out_shape=jax.ShapeDtypeStruct((out_rows, D_MODEL), jnp.float32),
        scratch_shapes=[
            pltpu.VMEM((TM, D_FF), jnp.bfloat16),
            pltpu.VMEM((SUBLANES, 2 * D_FF), jnp.float32),
            pltpu.VMEM((SUBLANES, 2 * D_FF), jnp.float32),
        ],
        compiler_params=_params(1),
        name="ffn_ln",
    )(h, win, cw_r, wout, g, b)


def _qkv_kernel(tiles_per_batch, h_ref, wq_ref, wkv_ref, wf_ref, bf_ref,
                q_ref, k_ref, v_ref, qa_ref, ka_ref, carry_ref, meta_c_ref):
    i = pl.program_id(0)

    @pl.when(i == 0)
    def _():
        carry_ref[...] = jnp.zeros_like(carry_ref)

    hb = h_ref[...].astype(jnp.bfloat16)
    z = jnp.dot(hb, wf_ref[...], preferred_element_type=jnp.float32) + bf_ref[...]
    q = jnp.dot(hb, wq_ref[...], preferred_element_type=jnp.float32)
    q_ref[...] = (q * (HEAD_DIM ** -0.5 * LOG2E)).astype(jnp.bfloat16)
    k_ref[...] = jnp.dot(hb, wkv_ref[:, :D_MODEL], preferred_element_type=jnp.float32).astype(jnp.bfloat16)
    v_ref[...] = jnp.dot(hb, wkv_ref[:, D_MODEL:], preferred_element_type=jnp.float32).astype(jnp.bfloat16)

    logf = jnp.minimum(z, 0.0) - jnp.log1p(jnp.exp(-jnp.abs(z)))
    row = lax.broadcasted_iota(jnp.int32, (TM, 1), 0)
    valid = jnp.logical_or(i > 0, row >= META_ROW0)
    x = jnp.where(valid, logf, 0.0)
    s = 1
    while s < TM:
        x = x + jnp.where(row >= s, pltpu.roll(x, s, axis=0), 0.0)
        s *= 2
    batch_start = jnp.logical_and(i > 0, (i - 1) % tiles_per_batch == 0)
    last = SUBLANES - 1
    base = jnp.where(batch_start, meta_c_ref[last:, :], carry_ref[last:, :])
    c = x + base
    tail = c[TM - SUBLANES:, :]
    carry_ref[...] = tail

    @pl.when(i == 0)
    def _():
        meta_c_ref[...] = tail

    c2 = c * LOG2E
    hi = c2.astype(jnp.bfloat16)
    r1 = c2 - hi.astype(jnp.float32)
    mid = r1.astype(jnp.bfloat16)
    lo = (r1 - mid.astype(jnp.float32)).astype(jnp.bfloat16)
    src = lax.broadcasted_iota(jnp.int32, (LANES, LANES), 0)
    dst = lax.broadcasted_iota(jnp.int32, (LANES, LANES), 1)

    def spread(piece, offset):
        sel = jnp.logical_and(dst == AUG * src + offset, src < N_HEADS).astype(jnp.bfloat16)
        return jnp.dot(piece, sel, preferred_element_type=jnp.float32)

    slot = lax.broadcasted_iota(jnp.int32, (1, LANES), 1) % AUG
    ones_q = jnp.logical_and(slot >= 3, slot < 6).astype(jnp.float32)
    ones_k = (slot < 3).astype(jnp.float32)
    qa_ref[...] = (spread(hi, 0) + spread(mid, 1) + spread(lo, 2) + ones_q).astype(jnp.bfloat16)
    ka_ref[...] = (ones_k - (spread(hi, 3) + spread(mid, 4) + spread(lo, 5))).astype(jnp.bfloat16)


def _qkv(h, wq, wkv, wf_pad, bf_pad, tiles_per_batch):
    rows = h.shape[0]
    n_tiles = rows // TM
    tile = pl.BlockSpec((TM, D_MODEL), lambda i: (i, 0))
    aug_tile = pl.BlockSpec((TM, LANES), lambda i: (i, 0))
    kv_pos = lambda i: (jnp.where(i == 0, n_tiles - 1, i - 1), 0)
    kv_tile = pl.BlockSpec((TM, D_MODEL), kv_pos)
    kv_aug_tile = pl.BlockSpec((TM, LANES), kv_pos)
    wide = jax.ShapeDtypeStruct((rows, D_MODEL), jnp.bfloat16)
    narrow = jax.ShapeDtypeStruct((rows, LANES), jnp.bfloat16)
    return pl.pallas_call(
        functools.partial(_qkv_kernel, tiles_per_batch),
        grid=(n_tiles,),
        in_specs=[tile, _const_spec(wq.shape), _const_spec(wkv.shape),
                  _const_spec(wf_pad.shape), _const_spec(bf_pad.shape)],
        out_specs=[tile, kv_tile, kv_tile, aug_tile, kv_aug_tile],
        out_shape=[wide, wide, wide, narrow, narrow],
        scratch_shapes=[pltpu.VMEM((SUBLANES, LANES), jnp.float32),
                        pltpu.VMEM((SUBLANES, LANES), jnp.float32)],
        compiler_params=_params(1),
        name="qkv_proj",
    )(h, wq, wkv, wf_pad, bf_pad)


KV_GROUP = 2
MODE_META, MODE_FULL, MODE_TAIL1, MODE_TAIL2 = 0, 1, 2, 3


def _attention_steps(n_tiles, tiles_per_batch):
    f_tab, kv_tab, first, last, mode, diag = [], [], [], [], [], []
    meta_block = (n_tiles - 1) // KV_GROUP
    for f in range(n_tiles):
        steps = [(meta_block, MODE_META)]
        if f > 0:
            b, qi = divmod(f - 1, tiles_per_batch)
            base = b * tiles_per_batch // KV_GROUP
            steps += [(base + j, MODE_FULL) for j in range(qi // 2)]
            steps.append((base + qi // 2, MODE_TAIL1 if qi % 2 == 0 else MODE_TAIL2))
        for n, (kv, md) in enumerate(steps):
            f_tab.append(f)
            kv_tab.append(kv)
            first.append(int(n == 0))
            last.append(int(n == len(steps) - 1))
            mode.append(md)
            diag.append(int(f == 0))
    return [np.asarray(t, np.int32) for t in (f_tab, kv_tab, first, last, mode, diag)]


def _attn_kernel(n_tiles, f_tab, kv_tab, first_tab, last_tab, mode_tab, diag_tab,
                 q_ref, qa_ref, k_ref, ka_ref, v_ref, h_ref, wo_ref, g_ref, b_ref, o_ref, m_ref, acc_ref, att_ref):
    del f_tab, kv_tab, first_tab
    step_id = pl.program_id(0)
    mode = mode_tab[step_id]

    lane = lax.broadcasted_iota(jnp.int32, (1, LANES), 1)
    low_half = lane < HEAD_DIM
    nt_dims = (((1,), (1,)), ((), ()))

    def step(subs):
        n_pairs = N_HEADS // 2
        qa = qa_ref[...]
        k0_of = {"meta": TM - LANES, "full": 0, "diag": 0}
        if any(kind == "meta" for _, kind in subs):
            qrow = lax.broadcasted_iota(jnp.int32, (TM, LANES), 0)
            krow = lax.broadcasted_iota(jnp.int32, (TM, LANES), 1) + (TM - LANES)
            causal = diag_tab[step_id] == 1
            keep = jnp.logical_and(krow >= META_ROW0, jnp.logical_or(krow <= qrow, jnp.logical_not(causal)))

        def pair_scores(item):
            sub, kind, p = item
            r0, r1 = sub * TM + k0_of[kind], (sub + 1) * TM
            cols = slice(p * LANES, (p + 1) * LANES)
            qp = q_ref[:, cols]
            k_ext = jnp.concatenate([k_ref[r0:r1, cols], ka_ref[r0:r1, :]], axis=1)
            vp = v_ref[r0:r1, cols]
            scores, vms = [], []
            for hh in range(2):
                hd = 2 * p + hh
                in_head = low_half if hh == 0 else jnp.logical_not(low_half)
                ones_lane = HEAD_DIM if hh == 0 else 0
                aug_lanes = jnp.logical_and(lane >= AUG * hd, lane < AUG * (hd + 1))
                q_ext = jnp.concatenate([jnp.where(in_head, qp, jnp.zeros_like(qp)),
                                         jnp.where(aug_lanes, qa, jnp.zeros_like(qa))], axis=1)
                vms.append(jnp.where(in_head, vp, jnp.where(lane == ones_lane, 1.0, 0.0).astype(vp.dtype)))
                scores.append(lax.dot_general(q_ext, k_ext, nt_dims, preferred_element_type=jnp.float32))
            return scores, vms

        def softmax_pv(kind, hd, s, vm):
            n_kblk = s.shape[1] // LANES
            if kind == "meta":
                s = jnp.where(keep, s, NEG_INF)
                m_new = jnp.broadcast_to(jnp.max(s, axis=1, keepdims=True), (TM, LANES))
                pe = jnp.exp2(s - m_new).astype(jnp.bfloat16)
                acc_ref[hd] = jnp.dot(pe, vm, preferred_element_type=jnp.float32)
                m_ref[hd] = m_new
                return
            m_prev = m_ref[hd]
            if kind == "diag":
                tri = (lax.broadcasted_iota(jnp.int32, (LANES, LANES), 1)
                       <= lax.broadcasted_iota(jnp.int32, (LANES, LANES), 0))
                blk = lambda qb, kb: s[qb * LANES:(qb + 1) * LANES, kb * LANES:(kb + 1) * LANES]
                diag_blk = [jnp.where(tri, blk(b, b), NEG_INF) for b in range(n_kblk)]
                m_rows = []
                for qb in range(n_kblk):
                    m_part = diag_blk[qb]
                    for kb in range(qb):
                        m_part = jnp.maximum(m_part, blk(qb, kb))
                    m_rows.append(jnp.max(m_part, axis=1, keepdims=True))
                m_new = jnp.maximum(m_prev, jnp.concatenate(m_rows, axis=0))
                p_rows = []
                for qb in range(n_kblk):
                    mq = m_new[qb * LANES:(qb + 1) * LANES, :]
                    p_rows.append(jnp.concatenate(
                        [jnp.exp2(blk(qb, kb) - mq) for kb in range(qb)]
                        + [jnp.exp2(diag_blk[qb] - mq)]
                        + [jnp.zeros((LANES, LANES), jnp.float32)] * (n_kblk - 1 - qb), axis=1))
                pe = jnp.concatenate(p_rows, axis=0).astype(jnp.bfloat16)
            else:
                m_part = s[:, :LANES]
                for kb in range(1, n_kblk):
                    m_part = jnp.maximum(m_part, s[:, kb * LANES:(kb + 1) * LANES])
                m_new = jnp.maximum(m_prev, jnp.max(m_part, axis=1, keepdims=True))
                pe = jnp.concatenate([jnp.exp2(s[:, kb * LANES:(kb + 1) * LANES] - m_new)
                                      for kb in range(n_kblk)], axis=1).astype(jnp.bfloat16)
            alpha = jnp.exp2(m_prev - m_new)
            pv = jnp.dot(pe, vm, preferred_element_type=jnp.float32)
            acc_ref[hd] = alpha * acc_ref[hd] + pv
            m_ref[hd] = m_new

        work = [(sub, kind, p) for sub, kind in subs for p in range(n_pairs)]
        cur = pair_scores(work[0])
        for n, (sub, kind, p) in enumerate(work):
            nxt = pair_scores(work[n + 1]) if n + 1 < len(work) else None
            for hh in range(2):
                softmax_pv(kind, 2 * p + hh, cur[0][hh], cur[1][hh])
            cur = nxt

    @pl.when(mode == MODE_META)
    def _():
        step([((n_tiles - 1) % KV_GROUP, "meta")])

    @pl.when(mode == MODE_FULL)
    def _():
        step([(0, "full"), (1, "full")])

    @pl.when(mode == MODE_TAIL1)
    def _():
        step([(0, "diag")])

    @pl.when(mode == MODE_TAIL2)
    def _():
        step([(1, "diag"), (0, "full")])

    @pl.when(last_tab[step_id] == 1)
    def _():
        for p in range(N_HEADS // 2):
            cols = slice(p * LANES, (p + 1) * LANES)
            a0 = acc_ref[2 * p]
            a1 = acc_ref[2 * p + 1]
            l0 = a0[:, HEAD_DIM:HEAD_DIM + 1]
            l1 = a1[:, 0:1]
            att_ref[:, cols] = jnp.where(low_half, a0 / l0, a1 / l1).astype(att_ref.dtype)
        mix = jnp.dot(att_ref[...], wo_ref[...], preferred_element_type=jnp.float32)
        o_ref[...] = _layer_norm(ALPHA * h_ref[...] + mix, g_ref[...], b_ref[...])


def _attention(q, qa, k, ka, v, h, wo, g, b, tiles_per_batch):
    rows = q.shape[0]
    assert tiles_per_batch % KV_GROUP == 0
    tabs = _attention_steps(rows // TM, tiles_per_batch)
    q_map = lambda s, f_tab, kv_tab, *_: (f_tab[s], 0)
    kv_map = lambda s, f_tab, kv_tab, *_: (kv_tab[s], 0)
    grid_spec = pltpu.PrefetchScalarGridSpec(
        num_scalar_prefetch=len(tabs),
        grid=(len(tabs[0]),),
        in_specs=[
            pl.BlockSpec((TM, D_MODEL), q_map),
            pl.BlockSpec((TM, LANES), q_map),
            pl.BlockSpec((KV_GROUP * TM, D_MODEL), kv_map),
            pl.BlockSpec((KV_GROUP * TM, LANES), kv_map),
            pl.BlockSpec((KV_GROUP * TM, D_MODEL), kv_map),
            pl.BlockSpec((TM, D_MODEL), q_map),
            _const_spec(wo.shape),
            _const_spec(g.shape),
            _const_spec(b.shape),
        ],
        out_specs=pl.BlockSpec((TM, D_MODEL), q_map),
        scratch_shapes=[
            pltpu.VMEM((N_HEADS, TM, LANES), jnp.float32),
            pltpu.VMEM((N_HEADS, TM, LANES), jnp.float32),
            pltpu.VMEM((TM, D_MODEL), jnp.bfloat16),
        ],
    )
    return pl.pallas_call(
        functools.partial(_attn_kernel, rows // TM),
        grid_spec=grid_spec,
        out_shape=jax.ShapeDtypeStruct((rows, D_MODEL), jnp.float32),
        compiler_params=_params(1),
        name="fox_attention_oproj_ln",
    )(*[jnp.asarray(t) for t in tabs], q, qa, k, ka, v, h, wo, g, b)


def _ffn_weights(w_in, conv_w, conv_b, w_out):
    F, FC, NC = D_FF, FFN_CHUNK, N_FFN_CHUNKS

    def regroup(m):
        a = m[..., :F].reshape(m.shape[:-1] + (NC, FC))
        g = m[..., F:].reshape(m.shape[:-1] + (NC, FC))
        return jnp.moveaxis(jnp.concatenate([a, g], axis=-1), -2, 0)

    win = w_in.astype(jnp.bfloat16)
    taps = jnp.concatenate([conv_w, conv_b[None, :],
                            jnp.zeros((SUBLANES - CONV_WIDTH - 1, 2 * F), conv_w.dtype)], axis=0)
    cw_r = regroup(taps)
    return win, cw_r, w_out.astype(jnp.bfloat16)


def kernel(x, meta, pool_w, pool_scale, w_kv, w_f, b_f, w_q, w_o, ffn_w_in, ffn_conv_w, ffn_conv_b,
           ffn_w_out, ln_g, ln_b):
    B, S, D = x.shape
    assert D == D_MODEL and S % TM == 0 and meta.shape == (N_META, D_MODEL)
    assert pool_w.shape[0] == N_A_LAYERS == 1 and w_q.shape[0] == DEPTH - N_A_LAYERS == 1
    tpb = S // TM

    meta_tile = jnp.concatenate([jnp.zeros((META_ROW0, D), x.dtype), meta.astype(x.dtype)], axis=0)
    row2 = lambda a: a.reshape(1, -1)

    h = _pool_ln(x.reshape(B * S, D), meta_tile, pool_w[0].astype(jnp.bfloat16), row2(pool_scale[0]),
                 row2(ln_g[0, 0]), row2(ln_b[0, 0]), tpb)
    h = _ffn_ln(h, *_ffn_weights(ffn_w_in[0], ffn_conv_w[0], ffn_conv_b[0], ffn_w_out[0]),
                row2(ln_g[0, 1]), row2(ln_b[0, 1]), tpb, drop_meta=False)

    wf_pad = jnp.pad(w_f, ((0, 0), (0, LANES - N_HEADS))).astype(jnp.bfloat16)
    bf_pad = jnp.pad(b_f, (0, LANES - N_HEADS)).reshape(1, LANES)
    q, k, v, qa, ka = _qkv(h, w_q[0].astype(jnp.bfloat16), w_kv.astype(jnp.bfloat16), wf_pad, bf_pad, tpb)
    h = _attention(q, qa, k, ka, v, h, w_o[0].astype(jnp.bfloat16), row2(ln_g[1, 0]), row2(ln_b[1, 0]), tpb)
    h = _ffn_ln(h, *_ffn_weights(ffn_w_in[1], ffn_conv_w[1], ffn_conv_b[1], ffn_w_out[1]),
                row2(ln_g[1, 1]), row2(ln_b[1, 1]), tpb, drop_meta=True)
    return h.reshape(B, S, D)
```
